```python
import jax, jax.numpy as jnp
from jax import lax
import numpy as np

D_MODEL = 1024
BATCH = 32
SEQ = 2048
DEPTH = 1

HEAD_DIM = 64
N_HEADS_A = 8
N_KV_HEADS_A = 2
GROUP_A = N_HEADS_A // N_KV_HEADS_A
N_HEADS_B = 8
WINDOW_A = 128
DILATED_PATTERNS = ((128, 1), (512, 4), (2048, 16))
ROPE_THETA = 500000.0
ROPE_DIM = HEAD_DIM // 4
D_FF = 4 * D_MODEL
BLOCK = 128
EPS = 1e-6

WIDTH_QA = N_HEADS_A * HEAD_DIM
WIDTH_KA = N_KV_HEADS_A * HEAD_DIM
WIDTH_B = N_HEADS_B * HEAD_DIM
MIX_WIDTH = WIDTH_QA + WIDTH_B
PROJ_WIDTH = WIDTH_QA + 2 * WIDTH_KA + 3 * WIDTH_B
SPLITS = (WIDTH_QA,
          WIDTH_QA + WIDTH_KA,
          WIDTH_QA + 2 * WIDTH_KA,
          WIDTH_QA + 2 * WIDTH_KA + WIDTH_B,
          WIDTH_QA + 2 * WIDTH_KA + 2 * WIDTH_B)

kernel_name = "hymba_swa_sink_dilated_hybrid"


def rms_norm(t, gain):
    t32 = t.astype(jnp.float32)
    y = t32 * lax.rsqrt(jnp.mean(t32 * t32, axis=-1, keepdims=True) + EPS)
    return (y * gain.astype(jnp.float32)).astype(t.dtype)


def rope_tables(positions):
    inv_freq = ROPE_THETA ** (-(jnp.arange(0, ROPE_DIM, 2, dtype=jnp.float32) / ROPE_DIM))
    ang = positions.astype(jnp.float32)[..., None] * inv_freq
    return jnp.cos(ang)[:, :, None, :], jnp.sin(ang)[:, :, None, :]


def apply_partial_rope(t, cos, sin):
    rot = t[..., :ROPE_DIM].astype(jnp.float32)
    x1, x2 = rot[..., :ROPE_DIM // 2], rot[..., ROPE_DIM // 2:]
    r = jnp.concatenate([x1 * cos - x2 * sin, x2 * cos + x1 * sin], axis=-1)
    return jnp.concatenate([r.astype(t.dtype), t[..., ROPE_DIM:]], axis=-1)


def banded_attention(q, k, v, max_dist, sinks=None):
    n, L, hkv, g, dh = q.shape
    nb = -(-L // BLOCK)
    pad = nb * BLOCK - L
    q = jnp.pad(q, ((0, 0), (0, pad), (0, 0), (0, 0), (0, 0)))
    kv_pad = ((0, 0), (BLOCK, pad), (0, 0), (0, 0))
    kb = jnp.pad(k, kv_pad).reshape(n, nb + 1, BLOCK, hkv, dh)
    vb = jnp.pad(v, kv_pad).reshape(n, nb + 1, BLOCK, hkv, dh)
    kw = jnp.concatenate([kb[:, :-1], kb[:, 1:]], axis=2)
    vw = jnp.concatenate([vb[:, :-1], vb[:, 1:]], axis=2)
    qb = q.reshape(n, nb, BLOCK, hkv, g, dh)
    s = jnp.einsum('nbqhgd,nbkhd->nbhgqk', qb, kw,
                   preferred_element_type=jnp.float32) * (dh ** -0.5)
    qi = jnp.arange(BLOCK)[:, None]
    kj = jnp.arange(2 * BLOCK)[None, :]
    dist = BLOCK + qi - kj
    band = (dist >= 0) & (dist <= max_dist)
    blk = jnp.arange(nb)[:, None, None]
    mask = band[None] & ((blk > 0) | (kj[None] >= BLOCK))
    s = jnp.where(mask[:, None, None], s, -jnp.inf)
    m = jnp.max(s, axis=-1)
    if sinks is not None:
        sink = sinks.astype(jnp.float32)[:, :, None]
        m = jnp.maximum(m, sink)
    p = jnp.exp(s - m[..., None])
    den = jnp.sum(p, axis=-1)
    lse = m + jnp.log(den)
    if sinks is not None:
        den = den + jnp.exp(sink - m)
    p = p / den[..., None]
    o = jnp.einsum('nbhgqk,nbkhd->nbqhgd', p.astype(v.dtype), vw)
    o = o.reshape(n, nb * BLOCK, hkv, g, dh)[:, :L]
    lse = jnp.transpose(lse, (0, 1, 4, 2, 3)).reshape(n, nb * BLOCK, hkv, g)[:, :L]
    return o, lse


def dilated_attention(q, k, v):
    b, s, h, dh = q.shape
    outs, lses = [], []
    for window, dil in DILATED_PATTERNS:
        sub = s // dil
        def to_sub(t):
            return t.reshape(b, sub, dil, h, dh).transpose(0, 2, 1, 3, 4).reshape(b * dil, sub, h, dh)
        o, lse = banded_attention(to_sub(q)[:, :, :, None], to_sub(k), to_sub(v), window // dil)
        outs.append(o[:, :, :, 0].reshape(b, dil, sub, h, dh).transpose(0, 2, 1, 3, 4).reshape(b, s, h, dh))
        lses.append(lse[..., 0].reshape(b, dil, sub, h).transpose(0, 2, 1, 3).reshape(b, s, h))
    wts = jax.nn.softmax(jnp.stack(lses, axis=0), axis=0)
    out = sum(wts[i][..., None] * outs[i].astype(jnp.float32) for i in range(len(outs)))
    return out.astype(q.dtype)


def setup_inputs(seed: int = 0) -> dict:
    key = jax.random.key(seed)
    ks = jax.random.split(key, 14)
    f32 = jnp.float32
    x = jax.random.normal(ks[0], (BATCH, SEQ, D_MODEL), f32)
    offsets = jax.random.randint(ks[1], (BATCH, 1), 0, 4096, dtype=jnp.int32)
    positions = offsets + jnp.arange(SEQ, dtype=jnp.int32)[None, :]
    gain = lambda k, n: 1.0 + 0.1 * jax.random.normal(k, (DEPTH, n), f32)
    return {
        "x": x,
        "positions": positions,
        "attn_norm_gain": gain(ks[2], D_MODEL),
        "w_in": jax.random.normal(ks[3], (DEPTH, D_MODEL, PROJ_WIDTH), f32) * D_MODEL ** -0.5,
        "q_norm_a": gain(ks[4], HEAD_DIM),
        "k_norm_a": gain(ks[5], HEAD_DIM),
        "sinks_a": 0.5 * jax.random.normal(ks[6], (DEPTH, N_HEADS_A), f32),
        "q_norm_b": gain(ks[7], HEAD_DIM),
        "k_norm_b": gain(ks[8], HEAD_DIM),
        "w_out": jax.random.normal(ks[9], (DEPTH, MIX_WIDTH, D_MODEL), f32) * MIX_WIDTH ** -0.5,
        "mlp_norm_gain": gain(ks[10], D_MODEL),
        "w_up": jax.random.normal(ks[11], (DEPTH, D_MODEL, D_FF), f32) * D_MODEL ** -0.5,
        "w_down": jax.random.normal(ks[12], (DEPTH, D_FF, D_MODEL), f32) * D_FF ** -0.5,
    }


def reference(x, positions, attn_norm_gain, w_in, q_norm_a, k_norm_a, sinks_a,
              q_norm_b, k_norm_b, w_out, mlp_norm_gain, w_up, w_down):
    b, s, _ = x.shape
    cos, sin = rope_tables(positions)
    h = x
    for layer in range(DEPTH):
        hn = rms_norm(h, attn_norm_gain[layer])
        proj = hn @ w_in[layer]
        qa, ka, va, qb, kb, vb = jnp.split(proj, SPLITS, axis=-1)
        qa = qa.reshape(b, s, N_HEADS_A, HEAD_DIM)
        ka = ka.reshape(b, s, N_KV_HEADS_A, HEAD_DIM)
        va = va.reshape(b, s, N_KV_HEADS_A, HEAD_DIM)
        qb = qb.reshape(b, s, N_HEADS_B, HEAD_DIM)
        kb = kb.reshape(b, s, N_HEADS_B, HEAD_DIM)
        vb = vb.reshape(b, s, N_HEADS_B, HEAD_DIM)
        qa = apply_partial_rope(rms_norm(qa, q_norm_a[layer]), cos, sin)
        ka = apply_partial_rope(rms_norm(ka, k_norm_a[layer]), cos, sin)
        qb = apply_partial_rope(rms_norm(qb, q_norm_b[layer]), cos, sin)
        kb = apply_partial_rope(rms_norm(kb, k_norm_b[layer]), cos, sin)
        oa, _ = banded_attention(qa.reshape(b, s, N_KV_HEADS_A, GROUP_A, HEAD_DIM), ka, va,
                                 WINDOW_A - 1, sinks_a[layer].reshape(N_KV_HEADS_A, GROUP_A))
        ob = dilated_attention(qb, kb, vb)
        mix = jnp.concatenate([oa.reshape(b, s, WIDTH_QA), ob.reshape(b, s, WIDTH_B)], axis=-1)
        h = h + mix @ w_out[layer]
        hn = rms_norm(h, mlp_norm_gain[layer])
        h = h + jnp.square(jax.nn.relu(hn @ w_up[layer])) @ w_down[layer]
    return h
```

```python
import functools

import numpy as np
import jax
import jax.numpy as jnp
from jax import lax
from jax.experimental import pallas as pl
from jax.experimental.pallas import tpu as pltpu

F32 = jnp.float32
BF16 = jnp.bfloat16

LANES = 128
HEAD_DIM = 64
PAIR = 2 * HEAD_DIM
N_HEADS_A = 8
N_KV_A = 2
N_HEADS_B = 8
WINDOW_A = 128
DILATIONS = (1, 4, 16)
BAND = 128
ROPE_DIM = HEAD_DIM // 4
ROPE_THETA = 500000.0
EPS = 1e-6
VMEM_LIMIT = 56 * 1024 * 1024

WQA = N_HEADS_A * HEAD_DIM
WKA = N_KV_A * HEAD_DIM
WB = N_HEADS_B * HEAD_DIM
QK_WIDTH = WQA + WKA + 2 * WB
N_PAIR_A = WQA // PAIR
N_PAIR_B = WB // PAIR
HEAD_ORDER_A = tuple(h for p in range(N_PAIR_A) for h in (p, p + N_HEADS_A // N_KV_A))


def _lane_lo(shape):
    return lax.broadcasted_iota(jnp.int32, shape, len(shape) - 1) < HEAD_DIM


def _proj_kernel(x_ref, pos_ref, gain_ref, w_ref, qkgain_ref, freq_ref, sign_ref,
                 qa_ref, ka_ref, va_ref, qb_ref, kb_ref, vb_ref):
    x = x_ref[0]
    ms = jnp.mean(x * x, axis=-1, keepdims=True)
    hn = (x * lax.rsqrt(ms + EPS) * gain_ref[...]).astype(BF16)
    proj = jnp.dot(hn, w_ref[...], preferred_element_type=F32)

    ang = pos_ref[0] * freq_ref[...]
    cos = jnp.cos(ang)
    sin = jnp.sin(ang) * sign_ref[...]
    tm = x.shape[0]
    lane = lax.broadcasted_iota(jnp.int32, (tm, LANES), 1)
    lo = lane < HEAD_DIM
    first_half = (lane & (HEAD_DIM - 1)) < (ROPE_DIM // 2)

    def qk_tile(c):
        p = proj[:, c * LANES:(c + 1) * LANES]
        p2 = p * p
        s_lo = jnp.sum(jnp.where(lo, p2, 0.0), axis=-1, keepdims=True)
        s_hi = jnp.sum(jnp.where(lo, 0.0, p2), axis=-1, keepdims=True)
        r = jnp.where(lo, lax.rsqrt(s_lo * (1.0 / HEAD_DIM) + EPS),
                      lax.rsqrt(s_hi * (1.0 / HEAD_DIM) + EPS))
        y = p * r * qkgain_ref[:, c * LANES:(c + 1) * LANES]
        partner = jnp.where(first_half,
                            pltpu.roll(y, LANES - ROPE_DIM // 2, 1),
                            pltpu.roll(y, ROPE_DIM // 2, 1))
        return y * cos + partner * sin

    c = 0
    for j in range(N_PAIR_A):
        qa_ref[0, :, j * LANES:(j + 1) * LANES] = qk_tile(c).astype(BF16)
        c += 1
    ka_ref[0] = qk_tile(c).astype(BF16)
    c += 1
    for j in range(N_PAIR_B):
        qb_ref[0, j] = qk_tile(c)
        c += 1
    for j in range(N_PAIR_B):
        kb_ref[0, j] = qk_tile(c)
        c += 1
    va_ref[0] = proj[:, c * LANES:(c + 1) * LANES].astype(BF16)
    c += 1
    for j in range(N_PAIR_B):
        vb_ref[0, j] = proj[:, c * LANES:(c + 1) * LANES]
        c += 1


def _split_heads(t):
    lo = _lane_lo(t.shape)
    zero = jnp.zeros_like(t)
    return jnp.where(lo, t, zero).astype(BF16), jnp.where(lo, zero, t).astype(BF16)


def _scores(q, k_lo, k_hi, mask):
    dn = (((1,), (1,)), ((), ()))
    s0 = lax.dot_general(q, k_lo, dn, preferred_element_type=F32)
    s1 = lax.dot_general(q, k_hi, dn, preferred_element_type=F32)
    return jnp.where(mask, s0, -jnp.inf), jnp.where(mask, s1, -jnp.inf)


def _band_mask(nq, nk, max_dist):
    qi = lax.broadcasted_iota(jnp.int32, (nq, nk), 0)
    kj = lax.broadcasted_iota(jnp.int32, (nq, nk), 1)
    dist = BAND + qi - kj
    return (dist >= 0) & (dist <= max_dist)


def _causal_mask(n):
    qi = lax.broadcasted_iota(jnp.int32, (n, n), 0)
    kj = lax.broadcasted_iota(jnp.int32, (n, n), 1)
    return kj <= qi


def _attn_a_kernel(sink_ref, q_ref, k_ref, v_ref, o_ref):
    seq = q_ref.shape[1]
    nblk = seq // BAND
    lo = _lane_lo((BAND, LANES))
    band = _band_mask(BAND, 2 * BAND, WINDOW_A - 1)
    causal = _causal_mask(BAND)

    def block(q_start, k_start, nk, mask):
        k_lo, k_hi = _split_heads(k_ref[0, pl.ds(k_start, nk), :])
        v_lo, v_hi = _split_heads(v_ref[0, pl.ds(k_start, nk), :])
        for p in range(N_PAIR_A):
            q = q_ref[0, pl.ds(q_start, BAND), p * LANES:(p + 1) * LANES]
            s0, s1 = _scores(q, k_lo, k_hi, mask)
            sink0 = sink_ref[HEAD_ORDER_A[2 * p]]
            sink1 = sink_ref[HEAD_ORDER_A[2 * p + 1]]
            m0 = jnp.maximum(jnp.max(s0, axis=-1, keepdims=True), sink0)
            m1 = jnp.maximum(jnp.max(s1, axis=-1, keepdims=True), sink1)
            p0 = jnp.exp(s0 - m0)
            p1 = jnp.exp(s1 - m1)
            den0 = jnp.sum(p0, axis=-1, keepdims=True) + jnp.exp(sink0 - m0)
            den1 = jnp.sum(p1, axis=-1, keepdims=True) + jnp.exp(sink1 - m1)
            acc = (jnp.dot(p0.astype(BF16), v_lo, preferred_element_type=F32)
                   + jnp.dot(p1.astype(BF16), v_hi, preferred_element_type=F32))
            out = acc / jnp.where(lo, den0, den1)
            o_ref[0, pl.ds(q_start, BAND), p * LANES:(p + 1) * LANES] = out.astype(o_ref.dtype)

    block(0, 0, BAND, causal)

    def body(b, carry):
        q_start = pl.multiple_of(b * BAND, BAND)
        k_start = pl.multiple_of((b - 1) * BAND, BAND)
        block(q_start, k_start, 2 * BAND, band)
        return carry

    lax.fori_loop(1, nblk, body, 0)


def _attn_b_kernel(q_ref, k_ref, v_ref, o_ref, m_ref, l_ref, acc_ref):
    seq = q_ref.shape[2]
    lo = _lane_lo((BAND, LANES))
    band = _band_mask(BAND, 2 * BAND, BAND)
    causal = _causal_mask(BAND)

    def partial(q_rows, k_rows, mask):
        q = q_ref[0, 0, q_rows, :].astype(BF16)
        k_lo, k_hi = _split_heads(k_ref[0, 0, k_rows, :])
        v_lo, v_hi = _split_heads(v_ref[0, 0, k_rows, :])
        s0, s1 = _scores(q, k_lo, k_hi, mask)
        m0 = jnp.max(s0, axis=-1, keepdims=True)
        m1 = jnp.max(s1, axis=-1, keepdims=True)
        p0 = jnp.exp(s0 - m0)
        p1 = jnp.exp(s1 - m1)
        l0 = jnp.sum(p0, axis=-1, keepdims=True)
        l1 = jnp.sum(p1, axis=-1, keepdims=True)
        acc = (jnp.dot(p0.astype(BF16), v_lo, preferred_element_type=F32)
               + jnp.dot(p1.astype(BF16), v_hi, preferred_element_type=F32))
        return jnp.where(lo, m0, m1), jnp.where(lo, l0, l1), acc

    def merged(q_rows, k_rows, mask):
        m_p, l_p, acc_p = partial(q_rows, k_rows, mask)
        m_old = m_ref[q_rows, :]
        m_new = jnp.maximum(m_old, m_p)
        a_old = jnp.exp(m_old - m_new)
        a_p = jnp.exp(m_p - m_new)
        l_new = l_ref[q_rows, :] * a_old + l_p * a_p
        acc_new = acc_ref[q_rows, :] * a_old + acc_p * a_p
        return m_new, l_new, acc_new

    d3 = DILATIONS[2]

    def body3(r, carry):
        rows = pl.ds(r, seq // d3, stride=d3)
        m_p, l_p, acc_p = partial(rows, rows, causal)
        m_ref[rows, :] = m_p
        l_ref[rows, :] = l_p
        acc_ref[rows, :] = acc_p
        return carry

    lax.fori_loop(0, d3, body3, 0)

    d2 = DILATIONS[1]
    nblk2 = seq // d2 // BAND

    def store(q_rows, state):
        m_ref[q_rows, :], l_ref[q_rows, :], acc_ref[q_rows, :] = state

    def body2_r(r, carry):
        rows0 = pl.ds(r, BAND, stride=d2)
        store(rows0, merged(rows0, rows0, causal))

        def body2_b(b, carry):
            q_rows = pl.ds(b * (BAND * d2) + r, BAND, stride=d2)
            k_rows = pl.ds((b - 1) * (BAND * d2) + r, 2 * BAND, stride=d2)
            store(q_rows, merged(q_rows, k_rows, band))
            return carry

        lax.fori_loop(1, nblk2, body2_b, 0)
        return carry

    lax.fori_loop(0, d2, body2_r, 0)

    def finish(q_rows, k_rows, mask):
        _, l_new, acc_new = merged(q_rows, k_rows, mask)
        o_ref[0, 0, q_rows, :] = (acc_new / l_new).astype(o_ref.dtype)

    finish(pl.ds(0, BAND), pl.ds(0, BAND), causal)

    def body1(b, carry):
        q_start = pl.multiple_of(b * BAND, BAND)
        k_start = pl.multiple_of((b - 1) * BAND, BAND)
        finish(pl.ds(q_start, BAND), pl.ds(k_start, 2 * BAND), band)
        return carry

    lax.fori_loop(1, seq // BAND, body1, 0)


def _mlp_kernel(x_ref, mixa_ref, mixb_ref, woa_ref, wob_ref, gain_ref, wup_ref, wdn_ref, o_ref,
                *, ff_chunk):
    mixb = jnp.concatenate([mixb_ref[0, j] for j in range(N_PAIR_B)], axis=-1)
    h = (x_ref[0]
         + jnp.dot(mixa_ref[0], woa_ref[...], preferred_element_type=F32)
         + jnp.dot(mixb, wob_ref[...], preferred_element_type=F32))
    ms = jnp.mean(h * h, axis=-1, keepdims=True)
    hn = (h * lax.rsqrt(ms + EPS) * gain_ref[...]).astype(BF16)
    mlp = None
    d_ff = wup_ref.shape[1]
    for c in range(d_ff // ff_chunk):
        u = jnp.dot(hn, wup_ref[:, c * ff_chunk:(c + 1) * ff_chunk], preferred_element_type=F32)
        a = jnp.square(jnp.maximum(u, 0.0)).astype(BF16)
        d = jnp.dot(a, wdn_ref[c * ff_chunk:(c + 1) * ff_chunk, :], preferred_element_type=F32)
        mlp = d if mlp is None else mlp + d
    o_ref[0] = h + mlp


def _const_spec(shape):
    nd = len(shape)
    return pl.BlockSpec(shape, lambda *_: (0,) * nd, pipeline_mode=pl.Buffered(1))


def _tok_spec(tm, width):
    return pl.BlockSpec((1, tm, width), lambda b, i: (b, i, 0))


def _pair_spec(tm):
    return pl.BlockSpec((1, N_PAIR_B, tm, LANES), lambda b, i: (b, 0, i, 0))


def _params(n_axes):
    return pltpu.CompilerParams(dimension_semantics=("parallel",) * n_axes,
                                vmem_limit_bytes=VMEM_LIMIT)


def _project(h, pos_tile, gain, w_in, qk_gain, freq_row, sign_row, tm):
    batch, seq, d_model = h.shape
    pair_shape = jax.ShapeDtypeStruct((batch, N_PAIR_B, seq, LANES), F32)
    return pl.pallas_call(
        _proj_kernel,
        grid=(batch, seq // tm),
        in_specs=[_tok_spec(tm, d_model), _tok_spec(tm, LANES), _const_spec((1, d_model)),
                  _const_spec(w_in.shape), _const_spec((1, QK_WIDTH)),
                  _const_spec((1, LANES)), _const_spec((1, LANES))],
        out_specs=[_tok_spec(tm, WQA), _tok_spec(tm, WKA), _tok_spec(tm, WKA),
                   _pair_spec(tm), _pair_spec(tm), _pair_spec(tm)],
        out_shape=[jax.ShapeDtypeStruct((batch, seq, WQA), BF16),
                   jax.ShapeDtypeStruct((batch, seq, WKA), BF16),
                   jax.ShapeDtypeStruct((batch, seq, WKA), BF16),
                   pair_shape, pair_shape, pair_shape],
        compiler_params=_params(2),
        name="proj_qknorm_rope",
    )(h, pos_tile, gain, w_in, qk_gain, freq_row, sign_row)


def _mixer_a(sinks, qa, ka, va):
    batch, seq, _ = qa.shape
    seq_spec = lambda w: pl.BlockSpec((1, seq, w), lambda b: (b, 0, 0))
    return pl.pallas_call(
        _attn_a_kernel,
        grid=(batch,),
        in_specs=[pl.BlockSpec(memory_space=pltpu.SMEM), seq_spec(WQA), seq_spec(WKA), seq_spec(WKA)],
        out_specs=seq_spec(WQA),
        out_shape=jax.ShapeDtypeStruct((batch, seq, WQA), BF16),
        compiler_params=_params(1),
        name="swa_gqa_sinks",
    )(sinks, qa, ka, va)


def _mixer_b(qb, kb, vb):
    batch, _, seq, _ = qb.shape
    head_spec = pl.BlockSpec((1, 1, seq, LANES), lambda b, p: (b, p, 0, 0))
    return pl.pallas_call(
        _attn_b_kernel,
        grid=(batch, N_PAIR_B),
        in_specs=[head_spec, head_spec, head_spec],
        out_specs=head_spec,
        out_shape=jax.ShapeDtypeStruct(qb.shape, BF16),
        scratch_shapes=[pltpu.VMEM((seq, LANES), F32)] * 3,
        compiler_params=_params(2),
        name="dilated_mixture",
    )(qb, kb, vb)


def _out_mlp(h, mix_a, mix_b, w_out_a, w_out_b, gain, w_up, w_dn, tm):
    batch, seq, d_model = h.shape
    return pl.pallas_call(
        functools.partial(_mlp_kernel, ff_chunk=1024),
        grid=(batch, seq // tm),
        in_specs=[_tok_spec(tm, d_model), _tok_spec(tm, WQA), _pair_spec(tm),
                  _const_spec(w_out_a.shape), _const_spec(w_out_b.shape),
                  _const_spec((1, d_model)), _const_spec(w_up.shape), _const_spec(w_dn.shape)],
        out_specs=_tok_spec(tm, d_model),
        out_shape=jax.ShapeDtypeStruct(h.shape, h.dtype),
        compiler_params=_params(2),
        name="outproj_mlp",
    )(h, mix_a, mix_b, w_out_a, w_out_b, gain, w_up, w_dn)


def kernel(x, positions, attn_norm_gain, w_in, q_norm_a, k_norm_a, sinks_a, q_norm_b, k_norm_b,
           w_out, mlp_norm_gain, w_up, w_down):
    batch, seq, d_model = x.shape
    assert seq // DILATIONS[2] == BAND
    tm = 512
    assert seq % tm == 0

    d = np.arange(LANES) % HEAD_DIM
    inv_freq = ROPE_THETA ** (-(np.arange(0, ROPE_DIM, 2, dtype=np.float32) / ROPE_DIM))
    freq_row = jnp.asarray(np.where(d < ROPE_DIM, inv_freq[d % (ROPE_DIM // 2)], 0.0)[None], F32)
    sign_row = jnp.asarray(np.where(d < ROPE_DIM // 2, -1.0, 1.0)[None], F32)
    pos_tile = jnp.broadcast_to(positions.astype(F32)[:, :, None], (batch, seq, LANES))

    o_ka, o_va = WQA, WQA + WKA
    o_qb, o_vb = WQA + 2 * WKA, WQA + 2 * WKA + 2 * WB
    qa_cols = np.concatenate([np.arange(h * HEAD_DIM, (h + 1) * HEAD_DIM) for h in HEAD_ORDER_A])
    col_order = np.concatenate([qa_cols, np.arange(o_ka, o_va), np.arange(o_qb, o_vb),
                                np.arange(o_va, o_qb), np.arange(o_vb, o_vb + WB)])
    scale = HEAD_DIM ** -0.5

    h = x
    for layer in range(w_in.shape[0]):
        qk_gain = jnp.concatenate([
            jnp.tile(q_norm_a[layer] * scale, N_HEADS_A), jnp.tile(k_norm_a[layer], N_KV_A),
            jnp.tile(q_norm_b[layer] * scale, N_HEADS_B), jnp.tile(k_norm_b[layer], N_HEADS_B)])[None]
        qa, ka, va, qb, kb, vb = _project(
            h, pos_tile, attn_norm_gain[layer][None], w_in[layer][:, col_order].astype(BF16),
            qk_gain, freq_row, sign_row, tm)
        mix_a = _mixer_a(sinks_a[layer].astype(F32), qa, ka, va)
        mix_b = _mixer_b(qb, kb, vb)
        h = _out_mlp(h, mix_a, mix_b, w_out[layer][qa_cols].astype(BF16),
                     w_out[layer][WQA:].astype(BF16), mlp_norm_gain[layer][None],
                     w_up[layer].astype(BF16), w_down[layer].astype(BF16), tm)
    return h
```

```python
import functools

import numpy as np
import jax
import jax.numpy as jnp
from jax import lax
from jax.experimental import pallas as pl
from jax.experimental.pallas import tpu as pltpu

F32 = jnp.float32
BF16 = jnp.bfloat16

LANES = 128
HEAD_DIM = 64
PAIR = 2 * HEAD_DIM
N_HEADS_A = 8
N_KV_A = 2
N_HEADS_B = 8
WINDOW_A = 128
DILATIONS = (1, 4, 16)
BAND = 128
ROPE_DIM = HEAD_DIM // 4
ROPE_THETA = 500000.0
EPS = 1e-6
LOG2E = 1.4426950408889634
GROUP_A = 3
GROUP_B = 4
GROUP_B1 = 3
VMEM_LIMIT = 56 * 1024 * 1024

WQA = N_HEADS_A * HEAD_DIM
WKA = N_KV_A * HEAD_DIM
WB = N_HEADS_B * HEAD_DIM
QK_WIDTH = WQA + WKA + 2 * WB
N_PAIR_A = WQA // PAIR
N_PAIR_B = WB // PAIR
HEAD_ORDER_A = tuple(h for p in range(N_PAIR_A) for h in (p, p + N_HEADS_A // N_KV_A))


def _lane_lo(shape):
    return lax.broadcasted_iota(jnp.int32, shape, len(shape) - 1) < HEAD_DIM


def _proj_kernel(x_ref, pos_ref, gain_ref, w_ref, qkgain_ref, freq_ref, sign_ref,
                 qa_ref, ka_ref, va_ref, qb_ref, kb_ref, vb_ref):
    x = x_ref[0]
    ms = jnp.mean(x * x, axis=-1, keepdims=True)
    hn = (x * lax.rsqrt(ms + EPS) * gain_ref[...]).astype(BF16)
    proj = jnp.dot(hn, w_ref[...], preferred_element_type=F32)

    ang = pos_ref[0] * freq_ref[...]
    cos = jnp.cos(ang)
    sin = jnp.sin(ang) * sign_ref[...]
    tm = x.shape[0]
    lane = lax.broadcasted_iota(jnp.int32, (tm, LANES), 1)
    lo = lane < HEAD_DIM
    first_half = (lane & (HEAD_DIM - 1)) < (ROPE_DIM // 2)

    def qk_tile(c):
        p = proj[:, c * LANES:(c + 1) * LANES]
        p2 = p * p
        s_lo = jnp.sum(jnp.where(lo, p2, 0.0), axis=-1, keepdims=True)
        s_hi = jnp.sum(jnp.where(lo, 0.0, p2), axis=-1, keepdims=True)
        r = jnp.where(lo, lax.rsqrt(s_lo * (1.0 / HEAD_DIM) + EPS),
                      lax.rsqrt(s_hi * (1.0 / HEAD_DIM) + EPS))
        y = p * r * qkgain_ref[:, c * LANES:(c + 1) * LANES]
        partner = jnp.where(first_half,
                            pltpu.roll(y, LANES - ROPE_DIM // 2, 1),
                            pltpu.roll(y, ROPE_DIM // 2, 1))
        return y * cos + partner * sin

    c = 0
    for j in range(N_PAIR_A):
        qa_ref[0, :, j * LANES:(j + 1) * LANES] = qk_tile(c).astype(BF16)
        c += 1
    ka_ref[0] = qk_tile(c).astype(BF16)
    c += 1
    for j in range(N_PAIR_B):
        qb_ref[0, j] = qk_tile(c)
        c += 1
    for j in range(N_PAIR_B):
        kb_ref[0, j] = qk_tile(c)
        c += 1
    va_ref[0] = proj[:, c * LANES:(c + 1) * LANES].astype(BF16)
    c += 1
    for j in range(N_PAIR_B):
        vb_ref[0, j] = proj[:, c * LANES:(c + 1) * LANES]
        c += 1


def _split_heads(t):
    lo = _lane_lo(t.shape)
    zero = jnp.zeros_like(t)
    return jnp.where(lo, t, zero).astype(BF16), jnp.where(lo, zero, t).astype(BF16)


def _scores(q, k_lo, k_hi, mask):
    dn = (((1,), (1,)), ((), ()))
    s0 = lax.dot_general(q, k_lo, dn, preferred_element_type=F32)
    s1 = lax.dot_general(q, k_hi, dn, preferred_element_type=F32)
    return jnp.where(mask, s0, -jnp.inf), jnp.where(mask, s1, -jnp.inf)


def _softmax_pair(s0, s1, floor0=None, floor1=None):
    m0 = jnp.max(s0, axis=-1, keepdims=True)
    m1 = jnp.max(s1, axis=-1, keepdims=True)
    if floor0 is not None:
        m0, m1 = jnp.maximum(m0, floor0), jnp.maximum(m1, floor1)
    p0 = jnp.exp2(s0 - m0)
    p1 = jnp.exp2(s1 - m1)
    l0 = jnp.sum(p0, axis=-1, keepdims=True)
    l1 = jnp.sum(p1, axis=-1, keepdims=True)
    return m0, m1, l0, l1, p0.astype(BF16), p1.astype(BF16)


def _weighted_values(p0, p1, v_lo, v_hi):
    return (jnp.dot(p0, v_lo, preferred_element_type=F32)
            + jnp.dot(p1, v_hi, preferred_element_type=F32))


def _band_mask(nq, nk, max_dist):
    qi = lax.broadcasted_iota(jnp.int32, (nq, nk), 0)
    kj = lax.broadcasted_iota(jnp.int32, (nq, nk), 1)
    dist = BAND + qi - kj
    return (dist >= 0) & (dist <= max_dist)


def _causal_mask(n):
    qi = lax.broadcasted_iota(jnp.int32, (n, n), 0)
    kj = lax.broadcasted_iota(jnp.int32, (n, n), 1)
    return kj <= qi


def _attn_a_kernel(sink_ref, q_ref, k_ref, v_ref, o_ref):
    seq = q_ref.shape[1]
    nblk = seq // BAND
    lo = _lane_lo((BAND, LANES))
    band = _band_mask(BAND, 2 * BAND, WINDOW_A - 1)
    causal = _causal_mask(BAND)
    sinks = [sink_ref[h] * LOG2E for h in HEAD_ORDER_A]

    def group(blocks):
        scores = []
        for q_start, k_start, nk, mask in blocks:
            k_lo, k_hi = _split_heads(k_ref[0, pl.ds(k_start, nk), :])
            for p in range(N_PAIR_A):
                q = q_ref[0, pl.ds(q_start, BAND), p * LANES:(p + 1) * LANES]
                scores.append(_scores(q, k_lo, k_hi, mask))
        probs = []
        for i, (s0, s1) in enumerate(scores):
            sink0, sink1 = sinks[2 * (i % N_PAIR_A)], sinks[2 * (i % N_PAIR_A) + 1]
            m0, m1, l0, l1, p0, p1 = _softmax_pair(s0, s1, sink0, sink1)
            den = jnp.where(lo, l0 + jnp.exp2(sink0 - m0), l1 + jnp.exp2(sink1 - m1))
            probs.append((p0, p1, den))
        for j, (q_start, k_start, nk, _) in enumerate(blocks):
            v_lo, v_hi = _split_heads(v_ref[0, pl.ds(k_start, nk), :])
            for p in range(N_PAIR_A):
                p0, p1, den = probs[j * N_PAIR_A + p]
                out = _weighted_values(p0, p1, v_lo, v_hi) / den
                o_ref[0, pl.ds(q_start, BAND), p * LANES:(p + 1) * LANES] = out.astype(o_ref.dtype)

    group([(0, 0, BAND, causal)])

    def body(i, carry):
        blocks = []
        for g in range(GROUP_A):
            b = 1 + i * GROUP_A + g
            blocks.append((pl.multiple_of(b * BAND, BAND), pl.multiple_of((b - 1) * BAND, BAND),
                           2 * BAND, band))
        group(blocks)
        return carry

    assert (nblk - 1) % GROUP_A == 0
    lax.fori_loop(0, (nblk - 1) // GROUP_A, body, 0)


def _attn_b_kernel(q_ref, k_ref, v_ref, o_ref, m2_ref, l2_ref, acc2_ref, m3_ref, l3_ref, acc3_ref):
    seq = q_ref.shape[2]
    lo = _lane_lo((BAND, LANES))
    band = _band_mask(BAND, 2 * BAND, BAND)
    causal = _causal_mask(BAND)

    def partials(blocks):
        scores = []
        for q_rows, k_rows, mask in blocks:
            q = q_ref[0, 0, q_rows, :].astype(BF16)
            k_lo, k_hi = _split_heads(k_ref[0, 0, k_rows, :])
            scores.append(_scores(q, k_lo, k_hi, mask))
        soft = [_softmax_pair(s0, s1) for s0, s1 in scores]
        states = []
        for (_, k_rows, _), (m0, m1, l0, l1, p0, p1) in zip(blocks, soft):
            v_lo, v_hi = _split_heads(v_ref[0, 0, k_rows, :])
            states.append((jnp.where(lo, m0, m1), jnp.where(lo, l0, l1),
                           _weighted_values(p0, p1, v_lo, v_hi)))
        return states

    d3 = DILATIONS[2]

    def body3(i, carry):
        rows = [pl.ds(i * GROUP_B + g, seq // d3, stride=d3) for g in range(GROUP_B)]
        for r, state in zip(rows, partials([(r, r, causal) for r in rows])):
            m3_ref[r, :], l3_ref[r, :], acc3_ref[r, :] = state
        return carry

    lax.fori_loop(0, d3 // GROUP_B, body3, 0)

    d2 = DILATIONS[1]

    def body2(r, carry):
        blocks = []
        for b in range(seq // d2 // BAND):
            q_rows = pl.ds(b * (BAND * d2) + r, BAND, stride=d2)
            if b == 0:
                blocks.append((q_rows, q_rows, causal))
            else:
                blocks.append((q_rows, pl.ds((b - 1) * (BAND * d2) + r, 2 * BAND, stride=d2), band))
        for (q_rows, _, _), state in zip(blocks, partials(blocks)):
            m2_ref[q_rows, :], l2_ref[q_rows, :], acc2_ref[q_rows, :] = state
        return carry

    lax.fori_loop(0, d2, body2, 0)

    def finish(blocks):
        for (q_rows, _, _), (m1, l1, acc1) in zip(blocks, partials(blocks)):
            m2, m3 = m2_ref[q_rows, :], m3_ref[q_rows, :]
            m = jnp.maximum(jnp.maximum(m1, m2), m3)
            a1, a2, a3 = jnp.exp2(m1 - m), jnp.exp2(m2 - m), jnp.exp2(m3 - m)
            den = l1 * a1 + l2_ref[q_rows, :] * a2 + l3_ref[q_rows, :] * a3
            acc = acc1 * a1 + acc2_ref[q_rows, :] * a2 + acc3_ref[q_rows, :] * a3
            o_ref[0, 0, q_rows, :] = (acc / den).astype(o_ref.dtype)

    finish([(pl.ds(0, BAND), pl.ds(0, BAND), causal)])

    def body1(i, carry):
        blocks = []
        for g in range(GROUP_B1):
            b = 1 + i * GROUP_B1 + g
            blocks.append((pl.ds(pl.multiple_of(b * BAND, BAND), BAND),
                           pl.ds(pl.multiple_of((b - 1) * BAND, BAND), 2 * BAND), band))
        finish(blocks)
        return carry

    assert d3 % GROUP_B == 0 and (seq // BAND - 1) % GROUP_B1 == 0
    lax.fori_loop(0, (seq // BAND - 1) // GROUP_B1, body1, 0)


def _mlp_kernel(x_ref, mixa_ref, mixb_ref, woa_ref, wob_ref, gain_ref, wup_ref, wdn_ref, o_ref,
                *, ff_chunk):
    mixb = jnp.concatenate([mixb_ref[0, j] for j in range(N_PAIR_B)], axis=-1)
    h = (x_ref[0]
         + jnp.dot(mixa_ref[0], woa_ref[...], preferred_element_type=F32)
         + jnp.dot(mixb, wob_ref[...], preferred_element_type=F32))
    ms = jnp.mean(h * h, axis=-1, keepdims=True)
    hn = (h * lax.rsqrt(ms + EPS) * gain_ref[...]).astype(BF16)
    mlp = None
    d_ff = wup_ref.shape[1]
    for c in range(d_ff // ff_chunk):
        u = jnp.dot(hn, wup_ref[:, c * ff_chunk:(c + 1) * ff_chunk], preferred_element_type=F32)
        a = jnp.square(jnp.maximum(u, 0.0)).astype(BF16)
        d = jnp.dot(a, wdn_ref[c * ff_chunk:(c + 1) * ff_chunk, :], preferred_element_type=F32)
        mlp = d if mlp is None else mlp + d
    o_ref[0] = h + mlp


def _const_spec(shape):
    nd = len(shape)
    return pl.BlockSpec(shape, lambda *_: (0,) * nd, pipeline_mode=pl.Buffered(1))


def _tok_spec(tm, width):
    return pl.BlockSpec((1, tm, width), lambda b, i: (b, i, 0))


def _pair_spec(tm):
    return pl.BlockSpec((1, N_PAIR_B, tm, LANES), lambda b, i: (b, 0, i, 0))


def _params(n_axes):
    return pltpu.CompilerParams(dimension_semantics=("parallel",) * n_axes,
                                vmem_limit_bytes=VMEM_LIMIT)


def _project(h, pos_tile, gain, w_in, qk_gain, freq_row, sign_row, tm):
    batch, seq, d_model = h.shape
    pair_shape = jax.ShapeDtypeStruct((batch, N_PAIR_B, seq, LANES), F32)
    return pl.pallas_call(
        _proj_kernel,
        grid=(batch, seq // tm),
        in_specs=[_tok_spec(tm, d_model), _tok_spec(tm, LANES), _const_spec((1, d_model)),
                  _const_spec(w_in.shape), _const_spec((1, QK_WIDTH)),
                  _const_spec((1, LANES)), _const_spec((1, LANES))],
        out_specs=[_tok_spec(tm, WQA), _tok_spec(tm, WKA), _tok_spec(tm, WKA),
                   _pair_spec(tm), _pair_spec(tm), _pair_spec(tm)],
        out_shape=[jax.ShapeDtypeStruct((batch, seq, WQA), BF16),
                   jax.ShapeDtypeStruct((batch, seq, WKA), BF16),
                   jax.ShapeDtypeStruct((batch, seq, WKA), BF16),
                   pair_shape, pair_shape, pair_shape],
        compiler_params=_params(2),
        name="proj_qknorm_rope",
    )(h, pos_tile, gain, w_in, qk_gain, freq_row, sign_row)


def _mixer_a(sinks, qa, ka, va):
    batch, seq, _ = qa.shape
    seq_spec = lambda w: pl.BlockSpec((1, seq, w), lambda b: (b, 0, 0))
    return pl.pallas_call(
        _attn_a_kernel,
        grid=(batch,),
        in_specs=[pl.BlockSpec(memory_space=pltpu.SMEM), seq_spec(WQA), seq_spec(WKA), seq_spec(WKA)],
        out_specs=seq_spec(WQA),
        out_shape=jax.ShapeDtypeStruct((batch, seq, WQA), BF16),
        compiler_params=_params(1),
        name="swa_gqa_sinks",
    )(sinks, qa, ka, va)


def _mixer_b(qb, kb, vb):
    batch, _, seq, _ = qb.shape
    head_spec = pl.BlockSpec((1, 1, seq, LANES), lambda b, p: (b, p, 0, 0))
    return pl.pallas_call(
        _attn_b_kernel,
        grid=(batch, N_PAIR_B),
        in_specs=[head_spec, head_spec, head_spec],
        out_specs=head_spec,
        out_shape=jax.ShapeDtypeStruct(qb.shape, BF16),
        scratch_shapes=[pltpu.VMEM((seq, LANES), F32)] * 6,
        compiler_params=_params(2),
        name="dilated_mixture",
    )(qb, kb, vb)


def _out_mlp(h, mix_a, mix_b, w_out_a, w_out_b, gain, w_up, w_dn, tm):
    batch, seq, d_model = h.shape
    return pl.pallas_call(
        functools.partial(_mlp_kernel, ff_chunk=1024),
        grid=(batch, seq // tm),
        in_specs=[_tok_spec(tm, d_model), _tok_spec(tm, WQA), _pair_spec(tm),
                  _const_spec(w_out_a.shape), _const_spec(w_out_b.shape),
                  _const_spec((1, d_model)), _const_spec(w_up.shape), _const_spec(w_dn.shape)],
        out_specs=_tok_spec(tm, d_model),
        out_shape=jax.ShapeDtypeStruct(h.shape, h.dtype),
        compiler_params=_params(2),
        name="outproj_mlp",
    )(h, mix_a, mix_b, w_out_a, w_out_b, gain, w_up, w_dn)


def kernel(x, positions, attn_norm_gain, w_in, q_norm_a, k_norm_a, sinks_a, q_norm_b, k_norm_b,
           w_out, mlp_norm_gain, w_up, w_down):
    batch, seq, d_model = x.shape
    assert seq // DILATIONS[2] == BAND
    tm = 512
    assert seq % tm == 0

    d = np.arange(LANES) % HEAD_DIM
    inv_freq = ROPE_THETA ** (-(np.arange(0, ROPE_DIM, 2, dtype=np.float32) / ROPE_DIM))
    freq_row = jnp.asarray(np.where(d < ROPE_DIM, inv_freq[d % (ROPE_DIM // 2)], 0.0)[None], F32)
    sign_row = jnp.asarray(np.where(d < ROPE_DIM // 2, -1.0, 1.0)[None], F32)
    pos_tile = jnp.broadcast_to(positions.astype(F32)[:, :, None], (batch, seq, LANES))

    o_ka, o_va = WQA, WQA + WKA
    o_qb, o_vb = WQA + 2 * WKA, WQA + 2 * WKA + 2 * WB
    qa_cols = np.concatenate([np.arange(h * HEAD_DIM, (h + 1) * HEAD_DIM) for h in HEAD_ORDER_A])
    col_order = np.concatenate([qa_cols, np.arange(o_ka, o_va), np.arange(o_qb, o_vb),
                                np.arange(o_va, o_qb), np.arange(o_vb, o_vb + WB)])
    scale = HEAD_DIM ** -0.5 * LOG2E

    h = x
    for layer in range(w_in.shape[0]):
        qk_gain = jnp.concatenate([
            jnp.tile(q_norm_a[layer] * scale, N_HEADS_A), jnp.tile(k_norm_a[layer], N_KV_A),
            jnp.tile(q_norm_b[layer] * scale, N_HEADS_B), jnp.tile(k_norm_b[layer], N_HEADS_B)])[None]
        qa, ka, va, qb, kb, vb = _project(
            h, pos_tile, attn_norm_gain[layer][None], w_in[layer][:, col_order].astype(BF16),
            qk_gain, freq_row, sign_row, tm)
        mix_a = _mixer_a(sinks_a[layer].astype(F32), qa, ka, va)
        mix_b = _mixer_b(qb, kb, vb)
        h = _out_mlp(h, mix_a, mix_b, w_out[layer][qa_cols].astype(BF16),
                     w_out[layer][WQA:].astype(BF16), mlp_norm_gain[layer][None],
                     w_up[layer].astype(BF16), w_down[layer].astype(BF16), tm)
    return h
```

```python
import functools

import numpy as np
import jax
import jax.numpy as jnp
from jax import lax
from jax.experimental import pallas as pl
from jax.experimental.pallas import tpu as pltpu

F32 = jnp.float32
BF16 = jnp.bfloat16

LANES = 128
HEAD_DIM = 64
PAIR = 2 * HEAD_DIM
N_HEADS_A = 8
N_KV_A = 2
N_HEADS_B = 8
WINDOW_A = 128
DILATIONS = (1, 4, 16)
BAND = 128
ROPE_DIM = HEAD_DIM // 4
ROPE_THETA = 500000.0
EPS = 1e-6
LOG2E = 1.4426950408889634
PROJ_CHUNK = 2
PROJ_DEPTH = 2
DEPTH_A = 4
DEPTH_B = 4
VMEM_LIMIT = 56 * 1024 * 1024

WQA = N_HEADS_A * HEAD_DIM
WKA = N_KV_A * HEAD_DIM
WB = N_HEADS_B * HEAD_DIM
QK_WIDTH = WQA + WKA + 2 * WB
N_PAIR_A = WQA // PAIR
N_PAIR_B = WB // PAIR
HEAD_ORDER_A = tuple(h for p in range(N_PAIR_A) for h in (p, p + N_HEADS_A // N_KV_A))


def _lane_lo(shape):
    return lax.broadcasted_iota(jnp.int32, shape, len(shape) - 1) < HEAD_DIM


def _proj_kernel(x_ref, pos_ref, gain_ref, w_ref, qkgain_ref, freq_ref, sign_ref, headsum_ref,
                 qa_ref, ka_ref, va_ref, qb_ref, kb_ref, vb_ref):
    x = x_ref[0]
    ms = jnp.mean(x * x, axis=-1, keepdims=True)
    hn = (x * lax.rsqrt(ms + EPS) * gain_ref[...]).astype(BF16)

    ang = pos_ref[0] * freq_ref[...]
    cos = jnp.cos(ang)
    sin = jnp.sin(ang) * sign_ref[...]
    tm = x.shape[0]
    lane = lax.broadcasted_iota(jnp.int32, (tm, LANES), 1)
    lo = lane < HEAD_DIM
    first_half = (lane & (HEAD_DIM - 1)) < (ROPE_DIM // 2)

    def qk_norm(p, c0):
        width = p.shape[1]
        mean_sq = jnp.dot((p * p).astype(BF16), headsum_ref[:width, :width],
                          preferred_element_type=F32)
        return p * lax.rsqrt(mean_sq + EPS) * qkgain_ref[:, c0 * LANES:c0 * LANES + width]

    def rope(y):
        partner = jnp.where(first_half,
                            pltpu.roll(y, LANES - ROPE_DIM // 2, 1),
                            pltpu.roll(y, ROPE_DIM // 2, 1))
        return y * cos + partner * sin

    def store_cols(ref, j):
        def store(t):
            ref[0, :, j * LANES:(j + 1) * LANES] = t.astype(ref.dtype)
        return store

    def store_pair(ref, j):
        def store(t):
            ref[0, j] = t
        return store

    def store_split(ref):
        def store(t):
            zero = jnp.zeros_like(t)
            ref[0, :, :LANES] = jnp.where(lo, t, zero).astype(ref.dtype)
            ref[0, :, LANES:] = jnp.where(lo, zero, t).astype(ref.dtype)
        return store

    n_qk = QK_WIDTH // LANES
    stores = ([store_cols(qa_ref, j) for j in range(N_PAIR_A)] + [store_split(ka_ref)]
              + [store_pair(qb_ref, j) for j in range(N_PAIR_B)]
              + [store_pair(kb_ref, j) for j in range(N_PAIR_B)] + [store_split(va_ref)]
              + [store_pair(vb_ref, j) for j in range(N_PAIR_B)])
    chunks = [(c0, min(PROJ_CHUNK, len(stores) - c0)) for c0 in range(0, len(stores), PROJ_CHUNK)]

    def product(chunk):
        c0, n = chunk
        return jnp.dot(hn, w_ref[:, c0 * LANES:(c0 + n) * LANES], preferred_element_type=F32)

    pending = [product(ch) for ch in chunks[:PROJ_DEPTH]]
    for i, (c0, n) in enumerate(chunks):
        p = pending.pop(0)
        nq = max(0, min(n, n_qk - c0))
        if nq:
            y = qk_norm(p[:, :nq * LANES], c0)
        if i + PROJ_DEPTH < len(chunks):
            pending.append(product(chunks[i + PROJ_DEPTH]))
        for t in range(n):
            cols = slice(t * LANES, (t + 1) * LANES)
            stores[c0 + t](rope(y[:, cols]) if t < nq else p[:, cols])


def _split_heads(t):
    lo = _lane_lo(t.shape)
    zero = jnp.zeros_like(t)
    return jnp.where(lo, t, zero).astype(BF16), jnp.where(lo, zero, t).astype(BF16)


def _scores(q, k_lo, k_hi, mask):
    dn = (((1,), (1,)), ((), ()))
    s0 = lax.dot_general(q, k_lo, dn, preferred_element_type=F32)
    s1 = lax.dot_general(q, k_hi, dn, preferred_element_type=F32)
    return jnp.where(mask, s0, -jnp.inf), jnp.where(mask, s1, -jnp.inf)


def _softmax_pair(s0, s1, floor0=None, floor1=None):
    m0 = jnp.max(s0, axis=-1, keepdims=True)
    m1 = jnp.max(s1, axis=-1, keepdims=True)
    if floor0 is not None:
        m0, m1 = jnp.maximum(m0, floor0), jnp.maximum(m1, floor1)
    p0 = jnp.exp2(s0 - m0)
    p1 = jnp.exp2(s1 - m1)
    l0 = jnp.sum(p0, axis=-1, keepdims=True)
    l1 = jnp.sum(p1, axis=-1, keepdims=True)
    return m0, m1, l0, l1, p0.astype(BF16), p1.astype(BF16)


def _weighted_values(p0, p1, v_lo, v_hi):
    return (jnp.dot(p0, v_lo, preferred_element_type=F32)
            + jnp.dot(p1, v_hi, preferred_element_type=F32))


def _band_mask(nq, nk, max_dist):
    qi = lax.broadcasted_iota(jnp.int32, (nq, nk), 0)
    kj = lax.broadcasted_iota(jnp.int32, (nq, nk), 1)
    dist = BAND + qi - kj
    return (dist >= 0) & (dist <= max_dist)


def _causal_mask(n):
    qi = lax.broadcasted_iota(jnp.int32, (n, n), 0)
    kj = lax.broadcasted_iota(jnp.int32, (n, n), 1)
    return kj <= qi


def _run_blocks(blocks, depth):
    def scores(block):
        load_q, load_k, _, mask = block[:4]
        return _scores(load_q().astype(BF16), *load_k(), mask)

    pending = [scores(b) for b in blocks[:depth]]
    for j, (_, _, load_v, _, floors, emit) in enumerate(blocks):
        m0, m1, l0, l1, p0, p1 = _softmax_pair(*pending.pop(0), *floors)
        acc = _weighted_values(p0, p1, *load_v())
        if j + depth < len(blocks):
            pending.append(scores(blocks[j + depth]))
        emit(m0, m1, l0, l1, acc)


def _attn_a_kernel(sink_ref, q_ref, k_ref, v_ref, o_ref):
    seq = q_ref.shape[1]
    lo = _lane_lo((BAND, LANES))
    band = _band_mask(BAND, 2 * BAND, WINDOW_A - 1)
    causal = _causal_mask(BAND)
    sinks = [sink_ref[h] * LOG2E for h in HEAD_ORDER_A]

    blocks = []
    for b in range(seq // BAND):
        q_rows = pl.ds(b * BAND, BAND)
        k_rows = q_rows if b == 0 else pl.ds((b - 1) * BAND, 2 * BAND)
        for p in range(N_PAIR_A):
            cols = slice(p * LANES, (p + 1) * LANES)
            sink0, sink1 = sinks[2 * p], sinks[2 * p + 1]

            def emit(m0, m1, l0, l1, acc, q_rows=q_rows, cols=cols, sink0=sink0, sink1=sink1):
                den = jnp.where(lo, l0, l1) + jnp.exp2(jnp.where(lo, sink0 - m0, sink1 - m1))
                o_ref[0, q_rows, cols] = (acc / den).astype(o_ref.dtype)

            blocks.append((lambda q_rows=q_rows, cols=cols: q_ref[0, q_rows, cols],
                           lambda k_rows=k_rows: (k_ref[0, k_rows, :LANES], k_ref[0, k_rows, LANES:]),
                           lambda k_rows=k_rows: (v_ref[0, k_rows, :LANES], v_ref[0, k_rows, LANES:]),
                           causal if b == 0 else band, (sink0, sink1), emit))
    _run_blocks(blocks, DEPTH_A)


def _attn_b_kernel(q_ref, k_ref, v_ref, q4_ref, k4_ref, v4_ref, q16_ref, k16_ref, v16_ref, o_ref,
                   m2_ref, l2_ref, acc2_ref, m3_ref, l3_ref, acc3_ref):
    seq = q_ref.shape[2]
    lo = _lane_lo((BAND, LANES))
    band = _band_mask(BAND, 2 * BAND, BAND)
    causal = _causal_mask(BAND)

    def block(refs, q_rows, k_rows, cols, mask, emit):
        q, k, v = refs
        return (lambda: q[0, 0, q_rows, cols], lambda: _split_heads(k[0, 0, k_rows, cols]),
                lambda: _split_heads(v[0, 0, k_rows, cols]), mask, (), emit)

    def store_to(m_ref, l_ref, acc_ref, rows):
        def emit(m0, m1, l0, l1, acc):
            m_ref[rows, :] = jnp.where(lo, m0, m1)
            l_ref[rows, :] = jnp.where(lo, l0, l1)
            acc_ref[rows, :] = acc
        return emit

    def finish(q_rows):
        def emit(m0, m1, l0, l1, acc1):
            m1t, l1t = jnp.where(lo, m0, m1), jnp.where(lo, l0, l1)
            m2, m3 = m2_ref[q_rows, :], m3_ref[q_rows, :]
            m = jnp.maximum(jnp.maximum(m1t, m2), m3)
            a1, a2, a3 = jnp.exp2(m1t - m), jnp.exp2(m2 - m), jnp.exp2(m3 - m)
            den = l1t * a1 + l2_ref[q_rows, :] * a2 + l3_ref[q_rows, :] * a3
            acc = acc1 * a1 + acc2_ref[q_rows, :] * a2 + acc3_ref[q_rows, :] * a3
            o_ref[0, 0, q_rows, :] = (acc / den).astype(o_ref.dtype)
        return emit

    d2, d3 = DILATIONS[1], DILATIONS[2]
    all_rows = slice(None)
    dil4, dil16, nat = (q4_ref, k4_ref, v4_ref), (q16_ref, k16_ref, v16_ref), (q_ref, k_ref, v_ref)

    blocks = []
    for r in range(d3):
        cols = slice(r * LANES, (r + 1) * LANES)
        blocks.append(block(dil16, all_rows, all_rows, cols, causal,
                            store_to(m3_ref, l3_ref, acc3_ref, pl.ds(r, seq // d3, stride=d3))))
    for b in range(seq // d2 // BAND):
        q_rows = pl.ds(b * BAND, BAND)
        k_rows = q_rows if b == 0 else pl.ds((b - 1) * BAND, 2 * BAND)
        for r in range(d2):
            cols = slice(r * LANES, (r + 1) * LANES)
            blocks.append(block(dil4, q_rows, k_rows, cols, causal if b == 0 else band,
                                store_to(m2_ref, l2_ref, acc2_ref,
                                         pl.ds(b * (BAND * d2) + r, BAND, stride=d2))))
    for b in range(seq // BAND):
        q_rows = pl.ds(b * BAND, BAND)
        k_rows = q_rows if b == 0 else pl.ds((b - 1) * BAND, 2 * BAND)
        blocks.append(block(nat, q_rows, k_rows, all_rows, causal if b == 0 else band,
                            finish(q_rows)))
    _run_blocks(blocks, DEPTH_B)


def _mlp_kernel(x_ref, mixa_ref, mixb_ref, woa_ref, wob_ref, gain_ref, wup_ref, wdn_ref, o_ref,
                *, ff_chunk):
    mixb = jnp.concatenate([mixb_ref[0, j] for j in range(N_PAIR_B)], axis=-1)
    h = (x_ref[0]
         + jnp.dot(mixa_ref[0], woa_ref[...], preferred_element_type=F32)
         + jnp.dot(mixb, wob_ref[...], preferred_element_type=F32))
    ms = jnp.mean(h * h, axis=-1, keepdims=True)
    hn = (h * lax.rsqrt(ms + EPS) * gain_ref[...]).astype(BF16)
    mlp = None
    d_ff = wup_ref.shape[1]
    for c in range(d_ff // ff_chunk):
        u = jnp.dot(hn, wup_ref[:, c * ff_chunk:(c + 1) * ff_chunk], preferred_element_type=F32)
        a = jnp.square(jnp.maximum(u, 0.0)).astype(BF16)
        d = jnp.dot(a, wdn_ref[c * ff_chunk:(c + 1) * ff_chunk, :], preferred_element_type=F32)
        mlp = d if mlp is None else mlp + d
    o_ref[0] = h + mlp


def _const_spec(shape):
    nd = len(shape)
    return pl.BlockSpec(shape, lambda *_: (0,) * nd, pipeline_mode=pl.Buffered(1))


def _tok_spec(tm, width):
    return pl.BlockSpec((1, tm, width), lambda b, i: (b, i, 0))


def _pair_spec(tm):
    return pl.BlockSpec((1, N_PAIR_B, tm, LANES), lambda b, i: (b, 0, i, 0))


def _params(n_axes):
    return pltpu.CompilerParams(dimension_semantics=("parallel",) * n_axes,
                                vmem_limit_bytes=VMEM_LIMIT)


def _project(h, pos_tile, gain, w_in, qk_gain, freq_row, sign_row, tm):
    batch, seq, d_model = h.shape
    pair_shape = jax.ShapeDtypeStruct((batch, N_PAIR_B, seq, LANES), F32)
    head = np.arange(2 * LANES) // HEAD_DIM
    head_sum = jnp.asarray((head[:, None] == head[None, :]) / HEAD_DIM, BF16)
    return pl.pallas_call(
        _proj_kernel,
        grid=(batch, seq // tm),
        in_specs=[_tok_spec(tm, d_model), _tok_spec(tm, LANES), _const_spec((1, d_model)),
                  _const_spec(w_in.shape), _const_spec((1, QK_WIDTH)),
                  _const_spec((1, LANES)), _const_spec((1, LANES)), _const_spec(head_sum.shape)],
        out_specs=[_tok_spec(tm, WQA), _tok_spec(tm, 2 * WKA), _tok_spec(tm, 2 * WKA),
                   _pair_spec(tm), _pair_spec(tm), _pair_spec(tm)],
        out_shape=[jax.ShapeDtypeStruct((batch, seq, WQA), BF16),
                   jax.ShapeDtypeStruct((batch, seq, 2 * WKA), BF16),
                   jax.ShapeDtypeStruct((batch, seq, 2 * WKA), BF16),
                   pair_shape, pair_shape, pair_shape],
        compiler_params=_params(2),
        name="proj_qknorm_rope",
    )(h, pos_tile, gain, w_in, qk_gain, freq_row, sign_row, head_sum)


def _mixer_a(sinks, qa, ka, va):
    batch, seq, _ = qa.shape
    seq_spec = lambda w: pl.BlockSpec((1, seq, w), lambda b: (b, 0, 0))
    return pl.pallas_call(
        _attn_a_kernel,
        grid=(batch,),
        in_specs=[pl.BlockSpec(memory_space=pltpu.SMEM), seq_spec(WQA), seq_spec(2 * WKA),
                  seq_spec(2 * WKA)],
        out_specs=seq_spec(WQA),
        out_shape=jax.ShapeDtypeStruct((batch, seq, WQA), BF16),
        compiler_params=_params(1),
        name="swa_gqa_sinks",
    )(sinks, qa, ka, va)


def _mixer_b(qb, kb, vb):
    batch, _, seq, _ = qb.shape
    views, specs = [], []
    for d in DILATIONS:
        shape = (1, 1, seq // d, d * LANES)
        views += [t.reshape(batch, N_PAIR_B, seq // d, d * LANES) for t in (qb, kb, vb)]
        specs += [pl.BlockSpec(shape, lambda b, p: (b, p, 0, 0))] * 3
    return pl.pallas_call(
        _attn_b_kernel,
        grid=(batch, N_PAIR_B),
        in_specs=specs,
        out_specs=specs[0],
        out_shape=jax.ShapeDtypeStruct(qb.shape, BF16),
        scratch_shapes=[pltpu.VMEM((seq, LANES), F32)] * 6,
        compiler_params=_params(2),
        name="dilated_mixture",
    )(*views)


def _out_mlp(h, mix_a, mix_b, w_out_a, w_out_b, gain, w_up, w_dn, tm):
    batch, seq, d_model = h.shape
    return pl.pallas_call(
        functools.partial(_mlp_kernel, ff_chunk=1024),
        grid=(batch, seq // tm),
        in_specs=[_tok_spec(tm, d_model), _tok_spec(tm, WQA), _pair_spec(tm),
                  _const_spec(w_out_a.shape), _const_spec(w_out_b.shape),
                  _const_spec((1, d_model)), _const_spec(w_up.shape), _const_spec(w_dn.shape)],
        out_specs=_tok_spec(tm, d_model),
        out_shape=jax.ShapeDtypeStruct(h.shape, h.dtype),
        compiler_params=_params(2),
        name="outproj_mlp",
    )(h, mix_a, mix_b, w_out_a, w_out_b, gain, w_up, w_dn)


def kernel(x, positions, attn_norm_gain, w_in, q_norm_a, k_norm_a, sinks_a, q_norm_b, k_norm_b,
           w_out, mlp_norm_gain, w_up, w_down):
    batch, seq, d_model = x.shape
    assert seq // DILATIONS[2] == BAND
    tm = 512
    assert seq % tm == 0

    d = np.arange(LANES) % HEAD_DIM
    inv_freq = ROPE_THETA ** (-(np.arange(0, ROPE_DIM, 2, dtype=np.float32) / ROPE_DIM))
    freq_row = jnp.asarray(np.where(d < ROPE_DIM, inv_freq[d % (ROPE_DIM // 2)], 0.0)[None], F32)
    sign_row = jnp.asarray(np.where(d < ROPE_DIM // 2, -1.0, 1.0)[None], F32)
    pos_tile = jnp.broadcast_to(positions.astype(F32)[:, :, None], (batch, seq, LANES))

    o_ka, o_va = WQA, WQA + WKA
    o_qb, o_vb = WQA + 2 * WKA, WQA + 2 * WKA + 2 * WB
    qa_cols = np.concatenate([np.arange(h * HEAD_DIM, (h + 1) * HEAD_DIM) for h in HEAD_ORDER_A])
    col_order = np.concatenate([qa_cols, np.arange(o_ka, o_va), np.arange(o_qb, o_vb),
                                np.arange(o_va, o_qb), np.arange(o_vb, o_vb + WB)])
    scale = HEAD_DIM ** -0.5 * LOG2E

    h = x
    for layer in range(w_in.shape[0]):
        qk_gain = jnp.concatenate([
            jnp.tile(q_norm_a[layer] * scale, N_HEADS_A), jnp.tile(k_norm_a[layer], N_KV_A),
            jnp.tile(q_norm_b[layer] * scale, N_HEADS_B), jnp.tile(k_norm_b[layer], N_HEADS_B)])[None]
        qa, ka, va, qb, kb, vb = _project(
            h, pos_tile, attn_norm_gain[layer][None], w_in[layer][:, col_order].astype(BF16),
            qk_gain, freq_row, sign_row, tm)
        mix_a = _mixer_a(sinks_a[layer].astype(F32), qa, ka, va)
        mix_b = _mixer_b(qb, kb, vb)
        h = _out_mlp(h, mix_a, mix_b, w_out[layer][qa_cols].astype(BF16),
                     w_out[layer][WQA:].astype(BF16), mlp_norm_gain[layer][None],
                     w_up[layer].astype(BF16), w_down[layer].astype(BF16), tm)
    return h
```

```python
import functools

import numpy as np
import jax
import jax.numpy as jnp
from jax import lax
from jax.experimental import pallas as pl
from jax.experimental.pallas import tpu as pltpu

F32 = jnp.float32
BF16 = jnp.bfloat16

LANES = 128
HEAD_DIM = 64
PAIR = 2 * HEAD_DIM
N_HEADS_A = 8
N_KV_A = 2
N_HEADS_B = 8
WINDOW_A = 128
DILATIONS = (1, 4, 16)
BAND = 128
ROPE_DIM = HEAD_DIM // 4
ROPE_THETA = 500000.0
EPS = 1e-6
LOG2E = 1.4426950408889634
PROJ_CHUNK = 2
PROJ_DEPTH = 2
DEPTH_A = 4
DEPTH_B = 4
VMEM_LIMIT = 56 * 1024 * 1024

WQA = N_HEADS_A * HEAD_DIM
WKA = N_KV_A * HEAD_DIM
WB = N_HEADS_B * HEAD_DIM
QK_WIDTH = WQA + WKA + 2 * WB
N_PAIR_A = WQA // PAIR
N_PAIR_B = WB // PAIR
HEAD_ORDER_A = tuple(h for p in range(N_PAIR_A) for h in (p, p + N_HEADS_A // N_KV_A))


def _lane_lo(shape):
    return lax.broadcasted_iota(jnp.int32, shape, len(shape) - 1) < HEAD_DIM


def _proj_kernel(x_ref, pos_ref, gain_ref, w_ref, qkgain_ref, freq_ref, sign_ref, headsum_ref,
                 qa_ref, ka_ref, va_ref, qb_ref, kb_ref, vb_ref):
    x = x_ref[0]
    ms = jnp.mean(x * x, axis=-1, keepdims=True)
    hn = (x * lax.rsqrt(ms + EPS) * gain_ref[...]).astype(BF16)

    ang = pos_ref[0] * freq_ref[...]
    cos = jnp.cos(ang)
    sin = jnp.sin(ang) * sign_ref[...]
    tm = x.shape[0]
    lane = lax.broadcasted_iota(jnp.int32, (tm, LANES), 1)
    lo = lane < HEAD_DIM
    first_half = (lane & (HEAD_DIM - 1)) < (ROPE_DIM // 2)

    def qk_norm(p, c0):
        width = p.shape[1]
        mean_sq = jnp.dot((p * p).astype(BF16), headsum_ref[:width, :width],
                          preferred_element_type=F32)
        return p * lax.rsqrt(mean_sq + EPS) * qkgain_ref[:, c0 * LANES:c0 * LANES + width]

    def rope(y):
        partner = jnp.where(first_half,
                            pltpu.roll(y, LANES - ROPE_DIM // 2, 1),
                            pltpu.roll(y, ROPE_DIM // 2, 1))
        return y * cos + partner * sin

    def store_cols(ref, j):
        def store(t):
            ref[0, :, j * LANES:(j + 1) * LANES] = t.astype(ref.dtype)
        return store

    def store_pair(ref, j):
        def store(t):
            ref[0, j] = t
        return store

    def store_split(ref):
        def store(t):
            zero = jnp.zeros_like(t)
            ref[0, :, :LANES] = jnp.where(lo, t, zero).astype(ref.dtype)
            ref[0, :, LANES:] = jnp.where(lo, zero, t).astype(ref.dtype)
        return store

    n_qk = QK_WIDTH // LANES
    stores = ([store_cols(qa_ref, j) for j in range(N_PAIR_A)] + [store_split(ka_ref)]
              + [store_pair(qb_ref, j) for j in range(N_PAIR_B)]
              + [store_pair(kb_ref, j) for j in range(N_PAIR_B)] + [store_split(va_ref)]
              + [store_pair(vb_ref, j) for j in range(N_PAIR_B)])
    chunks = [(c0, min(PROJ_CHUNK, len(stores) - c0)) for c0 in range(0, len(stores), PROJ_CHUNK)]

    def product(chunk):
        c0, n = chunk
        return jnp.dot(hn, w_ref[:, c0 * LANES:(c0 + n) * LANES], preferred_element_type=F32)

    pending = [product(ch) for ch in chunks[:PROJ_DEPTH]]
    for i, (c0, n) in enumerate(chunks):
        p = pending.pop(0)
        nq = max(0, min(n, n_qk - c0))
        if nq:
            y = qk_norm(p[:, :nq * LANES], c0)
        if i + PROJ_DEPTH < len(chunks):
            pending.append(product(chunks[i + PROJ_DEPTH]))
        for t in range(n):
            cols = slice(t * LANES, (t + 1) * LANES)
            stores[c0 + t](rope(y[:, cols]) if t < nq else p[:, cols])


def _split_heads(t):
    lo = _lane_lo(t.shape)
    zero = jnp.zeros_like(t)
    return jnp.where(lo, t, zero).astype(BF16), jnp.where(lo, zero, t).astype(BF16)


def _scores(q, k_lo, k_hi, mask):
    dn = (((1,), (1,)), ((), ()))
    s0 = lax.dot_general(q, k_lo, dn, preferred_element_type=F32)
    s1 = lax.dot_general(q, k_hi, dn, preferred_element_type=F32)
    return jnp.where(mask, s0, -jnp.inf), jnp.where(mask, s1, -jnp.inf)


def _softmax_pair(s0, s1, floor0=None, floor1=None):
    m0 = jnp.max(s0, axis=-1, keepdims=True)
    m1 = jnp.max(s1, axis=-1, keepdims=True)
    if floor0 is not None:
        m0, m1 = jnp.maximum(m0, floor0), jnp.maximum(m1, floor1)
    p0 = jnp.exp2(s0 - m0)
    p1 = jnp.exp2(s1 - m1)
    l0 = jnp.sum(p0, axis=-1, keepdims=True)
    l1 = jnp.sum(p1, axis=-1, keepdims=True)
    return m0, m1, l0, l1, p0.astype(BF16), p1.astype(BF16)


def _weighted_values(p0, p1, v_lo, v_hi):
    return (jnp.dot(p0, v_lo, preferred_element_type=F32)
            + jnp.dot(p1, v_hi, preferred_element_type=F32))


def _band_mask(nq, nk, max_dist):
    qi = lax.broadcasted_iota(jnp.int32, (nq, nk), 0)
    kj = lax.broadcasted_iota(jnp.int32, (nq, nk), 1)
    dist = BAND + qi - kj
    return (dist >= 0) & (dist <= max_dist)


def _causal_mask(n):
    qi = lax.broadcasted_iota(jnp.int32, (n, n), 0)
    kj = lax.broadcasted_iota(jnp.int32, (n, n), 1)
    return kj <= qi


def _run_blocks(blocks, depth):
    def scores(block):
        load_q, load_k, _, mask = block[:4]
        return _scores(load_q().astype(BF16), *load_k(), mask)

    pending = [scores(b) for b in blocks[:depth]]
    for j, (_, _, load_v, _, floors, emit) in enumerate(blocks):
        m0, m1, l0, l1, p0, p1 = _softmax_pair(*pending.pop(0), *floors)
        acc = _weighted_values(p0, p1, *load_v())
        if j + depth < len(blocks):
            pending.append(scores(blocks[j + depth]))
        emit(m0, m1, l0, l1, acc)


def _attn_a_kernel(sink_ref, q_ref, k_ref, v_ref, o_ref):
    seq = q_ref.shape[1]
    lo = _lane_lo((BAND, LANES))
    band = _band_mask(BAND, 2 * BAND, WINDOW_A - 1)
    causal = _causal_mask(BAND)
    sinks = [sink_ref[h] * LOG2E for h in HEAD_ORDER_A]

    blocks = []
    for b in range(seq // BAND):
        q_rows = pl.ds(b * BAND, BAND)
        k_rows = q_rows if b == 0 else pl.ds((b - 1) * BAND, 2 * BAND)
        for p in range(N_PAIR_A):
            cols = slice(p * LANES, (p + 1) * LANES)
            sink0, sink1 = sinks[2 * p], sinks[2 * p + 1]

            def emit(m0, m1, l0, l1, acc, q_rows=q_rows, cols=cols, sink0=sink0, sink1=sink1):
                den = jnp.where(lo, l0, l1) + jnp.exp2(jnp.where(lo, sink0 - m0, sink1 - m1))
                o_ref[0, q_rows, cols] = (acc / den).astype(o_ref.dtype)

            blocks.append((lambda q_rows=q_rows, cols=cols: q_ref[0, q_rows, cols],
                           lambda k_rows=k_rows: (k_ref[0, k_rows, :LANES], k_ref[0, k_rows, LANES:]),
                           lambda k_rows=k_rows: (v_ref[0, k_rows, :LANES], v_ref[0, k_rows, LANES:]),
                           causal if b == 0 else band, (sink0, sink1), emit))
    _run_blocks(blocks, DEPTH_A)


def _attn_b_kernel(q_ref, k_ref, v_ref, o_ref, m2_ref, l2_ref, acc2_ref, m3_ref, l3_ref, acc3_ref):
    seq = q_ref.shape[2]
    lo = _lane_lo((BAND, LANES))
    band = _band_mask(BAND, 2 * BAND, BAND)
    causal = _causal_mask(BAND)

    def block(q_rows, k_rows, mask, emit):
        return (lambda: q_ref[0, 0, q_rows, :], lambda: _split_heads(k_ref[0, 0, k_rows, :]),
                lambda: _split_heads(v_ref[0, 0, k_rows, :]), mask, (), emit)

    def store_to(m_ref, l_ref, acc_ref, rows):
        def emit(m0, m1, l0, l1, acc):
            m_ref[rows, :] = jnp.where(lo, m0, m1)
            l_ref[rows, :] = jnp.where(lo, l0, l1)
            acc_ref[rows, :] = acc
        return emit

    def finish(q_rows):
        def emit(m0, m1, l0, l1, acc1):
            m1t, l1t = jnp.where(lo, m0, m1), jnp.where(lo, l0, l1)
            m2, m3 = m2_ref[q_rows, :], m3_ref[q_rows, :]
            m = jnp.maximum(jnp.maximum(m1t, m2), m3)
            a1, a2, a3 = jnp.exp2(m1t - m), jnp.exp2(m2 - m), jnp.exp2(m3 - m)
            den = l1t * a1 + l2_ref[q_rows, :] * a2 + l3_ref[q_rows, :] * a3
            acc = acc1 * a1 + acc2_ref[q_rows, :] * a2 + acc3_ref[q_rows, :] * a3
            o_ref[0, 0, q_rows, :] = (acc / den).astype(o_ref.dtype)
        return emit

    d2, d3 = DILATIONS[1], DILATIONS[2]

    blocks = []
    for r in range(d3):
        rows = pl.ds(r, seq // d3, stride=d3)
        blocks.append(block(rows, rows, causal, store_to(m3_ref, l3_ref, acc3_ref, rows)))
    for b in range(seq // d2 // BAND):
        for r in range(d2):
            q_rows = pl.ds(b * (BAND * d2) + r, BAND, stride=d2)
            k_rows = q_rows if b == 0 else pl.ds((b - 1) * (BAND * d2) + r, 2 * BAND, stride=d2)
            blocks.append(block(q_rows, k_rows, causal if b == 0 else band,
                                store_to(m2_ref, l2_ref, acc2_ref, q_rows)))
    for b in range(seq // BAND):
        q_rows = pl.ds(b * BAND, BAND)
        k_rows = q_rows if b == 0 else pl.ds((b - 1) * BAND, 2 * BAND)
        blocks.append(block(q_rows, k_rows, causal if b == 0 else band, finish(q_rows)))
    _run_blocks(blocks, DEPTH_B)


def _mlp_kernel(x_ref, mixa_ref, mixb_ref, woa_ref, wob_ref, gain_ref, wup_ref, wdn_ref, o_ref,
                *, ff_chunk):
    mixb = jnp.concatenate([mixb_ref[0, j] for j in range(N_PAIR_B)], axis=-1)
    h = (x_ref[0]
         + jnp.dot(mixa_ref[0], woa_ref[...], preferred_element_type=F32)
         + jnp.dot(mixb, wob_ref[...], preferred_element_type=F32))
    ms = jnp.mean(h * h, axis=-1, keepdims=True)
    hn = (h * lax.rsqrt(ms + EPS) * gain_ref[...]).astype(BF16)
    mlp = None
    d_ff = wup_ref.shape[1]
    for c in range(d_ff // ff_chunk):
        u = jnp.dot(hn, wup_ref[:, c * ff_chunk:(c + 1) * ff_chunk], preferred_element_type=F32)
        a = jnp.square(jnp.maximum(u, 0.0)).astype(BF16)
        d = jnp.dot(a, wdn_ref[c * ff_chunk:(c + 1) * ff_chunk, :], preferred_element_type=F32)
        mlp = d if mlp is None else mlp + d
    o_ref[0] = h + mlp


def _const_spec(shape):
    nd = len(shape)
    return pl.BlockSpec(shape, lambda *_: (0,) * nd, pipeline_mode=pl.Buffered(1))


def _tok_spec(tm, width):
    return pl.BlockSpec((1, tm, width), lambda b, i: (b, i, 0))


def _pair_spec(tm):
    return pl.BlockSpec((1, N_PAIR_B, tm, LANES), lambda b, i: (b, 0, i, 0))


def _params(n_axes):
    return pltpu.CompilerParams(dimension_semantics=("parallel",) * n_axes,
                                vmem_limit_bytes=VMEM_LIMIT)


def _project(h, pos_tile, gain, w_in, qk_gain, freq_row, sign_row, tm):
    batch, seq, d_model = h.shape
    pair_shape = jax.ShapeDtypeStruct((batch, N_PAIR_B, seq, LANES), F32)
    head = np.arange(2 * LANES) // HEAD_DIM
    head_sum = jnp.asarray((head[:, None] == head[None, :]) / HEAD_DIM, BF16)
    return pl.pallas_call(
        _proj_kernel,
        grid=(batch, seq // tm),
        in_specs=[_tok_spec(tm, d_model), _tok_spec(tm, LANES), _const_spec((1, d_model)),
                  _const_spec(w_in.shape), _const_spec((1, QK_WIDTH)),
                  _const_spec((1, LANES)), _const_spec((1, LANES)), _const_spec(head_sum.shape)],
        out_specs=[_tok_spec(tm, WQA), _tok_spec(tm, 2 * WKA), _tok_spec(tm, 2 * WKA),
                   _pair_spec(tm), _pair_spec(tm), _pair_spec(tm)],
        out_shape=[jax.ShapeDtypeStruct((batch, seq, WQA), BF16),
                   jax.ShapeDtypeStruct((batch, seq, 2 * WKA), BF16),
                   jax.ShapeDtypeStruct((batch, seq, 2 * WKA), BF16),
                   pair_shape, pair_shape, pair_shape],
        compiler_params=_params(2),
        name="proj_qknorm_rope",
    )(h, pos_tile, gain, w_in, qk_gain, freq_row, sign_row, head_sum)


def _mixer_a(sinks, qa, ka, va):
    batch, seq, _ = qa.shape
    seq_spec = lambda w: pl.BlockSpec((1, seq, w), lambda b: (b, 0, 0))
    return pl.pallas_call(
        _attn_a_kernel,
        grid=(batch,),
        in_specs=[pl.BlockSpec(memory_space=pltpu.SMEM), seq_spec(WQA), seq_spec(2 * WKA),
                  seq_spec(2 * WKA)],
        out_specs=seq_spec(WQA),
        out_shape=jax.ShapeDtypeStruct((batch, seq, WQA), BF16),
        compiler_params=_params(1),
        name="swa_gqa_sinks",
    )(sinks, qa, ka, va)


def _mixer_b(qb, kb, vb):
    batch, _, seq, _ = qb.shape
    head_spec = pl.BlockSpec((1, 1, seq, LANES), lambda b, p: (b, p, 0, 0))
    return pl.pallas_call(
        _attn_b_kernel,
        grid=(batch, N_PAIR_B),
        in_specs=[head_spec, head_spec, head_spec],
        out_specs=head_spec,
        out_shape=jax.ShapeDtypeStruct(qb.shape, BF16),
        scratch_shapes=[pltpu.VMEM((seq, LANES), F32)] * 6,
        compiler_params=_params(2),
        name="dilated_mixture",
    )(qb, kb, vb)


def _out_mlp(h, mix_a, mix_b, w_out_a, w_out_b, gain, w_up, w_dn, tm):
    batch, seq, d_model = h.shape
    return pl.pallas_call(
        functools.partial(_mlp_kernel, ff_chunk=1024),
        grid=(batch, seq // tm),
        in_specs=[_tok_spec(tm, d_model), _tok_spec(tm, WQA), _pair_spec(tm),
                  _const_spec(w_out_a.shape), _const_spec(w_out_b.shape),
                  _const_spec((1, d_model)), _const_spec(w_up.shape), _const_spec(w_dn.shape)],
        out_specs=_tok_spec(tm, d_model),
        out_shape=jax.ShapeDtypeStruct(h.shape, h.dtype),
        compiler_params=_params(2),
        name="outproj_mlp",
    )(h, mix_a, mix_b, w_out_a, w_out_b, gain, w_up, w_dn)


def kernel(x, positions, attn_norm_gain, w_in, q_norm_a, k_norm_a, sinks_a, q_norm_b, k_norm_b,
           w_out, mlp_norm_gain, w_up, w_down):
    batch, seq, d_model = x.shape
    assert seq // DILATIONS[2] == BAND
    tm = 512
    assert seq % tm == 0

    d = np.arange(LANES) % HEAD_DIM
    inv_freq = ROPE_THETA ** (-(np.arange(0, ROPE_DIM, 2, dtype=np.float32) / ROPE_DIM))
    freq_row = jnp.asarray(np.where(d < ROPE_DIM, inv_freq[d % (ROPE_DIM // 2)], 0.0)[None], F32)
    sign_row = jnp.asarray(np.where(d < ROPE_DIM // 2, -1.0, 1.0)[None], F32)
    pos_tile = jnp.broadcast_to(positions.astype(F32)[:, :, None], (batch, seq, LANES))

    o_ka, o_va = WQA, WQA + WKA
    o_qb, o_vb = WQA + 2 * WKA, WQA + 2 * WKA + 2 * WB
    qa_cols = np.concatenate([np.arange(h * HEAD_DIM, (h + 1) * HEAD_DIM) for h in HEAD_ORDER_A])
    col_order = np.concatenate([qa_cols, np.arange(o_ka, o_va), np.arange(o_qb, o_vb),
                                np.arange(o_va, o_qb), np.arange(o_vb, o_vb + WB)])
    scale = HEAD_DIM ** -0.5 * LOG2E

    h = x
    for layer in range(w_in.shape[0]):
        qk_gain = jnp.concatenate([
            jnp.tile(q_norm_a[layer] * scale, N_HEADS_A), jnp.tile(k_norm_a[layer], N_KV_A),
            jnp.tile(q_norm_b[layer] * scale, N_HEADS_B), jnp.tile(k_norm_b[layer], N_HEADS_B)])[None]
        qa, ka, va, qb, kb, vb = _project(
            h, pos_tile, attn_norm_gain[layer][None], w_in[layer][:, col_order].astype(BF16),
            qk_gain, freq_row, sign_row, tm)
        mix_a = _mixer_a(sinks_a[layer].astype(F32), qa, ka, va)
        mix_b = _mixer_b(qb, kb, vb)
        h = _out_mlp(h, mix_a, mix_b, w_out[layer][qa_cols].astype(BF16),
                     w_out[layer][WQA:].astype(BF16), mlp_norm_gain[layer][None],
                     w_up[layer].astype(BF16), w_down[layer].astype(BF16), tm)
    return h
```

```python
import functools

import numpy as np
import jax
import jax.numpy as jnp
from jax import lax
from jax.experimental import pallas as pl
from jax.experimental.pallas import tpu as pltpu

F32 = jnp.float32
BF16 = jnp.bfloat16

LANES = 128
HEAD_DIM = 64
PAIR = 2 * HEAD_DIM
N_HEADS_A = 8
N_KV_A = 2
N_HEADS_B = 8
WINDOW_A = 128
DILATIONS = (1, 4, 16)
BAND = 128
ROPE_DIM = HEAD_DIM // 4
ROPE_THETA = 500000.0
EPS = 1e-6
LOG2E = 1.4426950408889634
PROJ_CHUNK = 2
PROJ_DEPTH = 2
DEPTH_A = 4
DEPTH_B = 4
VMEM_LIMIT = 56 * 1024 * 1024

WQA = N_HEADS_A * HEAD_DIM
WKA = N_KV_A * HEAD_DIM
WB = N_HEADS_B * HEAD_DIM
QK_WIDTH = WQA + WKA + 2 * WB
N_PAIR_A = WQA // PAIR
N_PAIR_B = WB // PAIR
HEAD_ORDER_A = tuple(h for p in range(N_PAIR_A) for h in (p, p + N_HEADS_A // N_KV_A))


def _lane_lo(shape):
    return lax.broadcasted_iota(jnp.int32, shape, len(shape) - 1) < HEAD_DIM


def _rope_table_kernel(pos_ref, freq_ref, o_ref):
    ang = freq_ref[...] * pos_ref[0].astype(F32)
    cs = jnp.concatenate([jnp.cos(ang), jnp.sin(ang)], axis=0)
    high = cs.astype(BF16)
    rest = cs - high.astype(F32)
    middle = rest.astype(BF16)
    low = (rest - middle.astype(F32)).astype(BF16)
    row = lax.broadcasted_iota(jnp.int32, cs.shape, 0)
    ones_row = jnp.where(row == 0, 1.0, 0.0).astype(BF16)
    o_ref[0] = jnp.concatenate([high, middle, low, ones_row], axis=0)


def _proj_kernel(x_ref, rope_ref, gain_ref, w_ref, qkgain_ref, ropesel_ref, headsum_ref,
                 qa_ref, ka_ref, va_ref, qb_ref, kb_ref, vb_ref):
    x = x_ref[0]
    ms = jnp.mean(x * x, axis=-1, keepdims=True)
    hn = (x * lax.rsqrt(ms + EPS) * gain_ref[...]).astype(BF16)

    rope = jnp.dot(rope_ref[0], ropesel_ref[...], preferred_element_type=F32)
    cos, sin = rope[:, :LANES], rope[:, LANES:]
    tm = x.shape[0]
    lane = lax.broadcasted_iota(jnp.int32, (tm, LANES), 1)
    lo = lane < HEAD_DIM
    first_half = (lane & (HEAD_DIM - 1)) < (ROPE_DIM // 2)

    def qk_norm(p, c0):
        width = p.shape[1]
        mean_sq = jnp.dot((p * p).astype(BF16), headsum_ref[:width, :width],
                          preferred_element_type=F32)
        return p * lax.rsqrt(mean_sq + EPS) * qkgain_ref[:, c0 * LANES:c0 * LANES + width]

    def rope(y):
        partner = jnp.where(first_half,
                            pltpu.roll(y, LANES - ROPE_DIM // 2, 1),
                            pltpu.roll(y, ROPE_DIM // 2, 1))
        return y * cos + partner * sin

    def store_cols(ref, j):
        def store(t):
            ref[0, :, j * LANES:(j + 1) * LANES] = t.astype(ref.dtype)
        return store

    def store_pair(ref, j):
        def store(t):
            ref[0, j] = t
        return store

    def store_split(ref):
        def store(t):
            zero = jnp.zeros_like(t)
            ref[0, :, :LANES] = jnp.where(lo, t, zero).astype(ref.dtype)
            ref[0, :, LANES:] = jnp.where(lo, zero, t).astype(ref.dtype)
        return store

    n_qk = QK_WIDTH // LANES
    stores = ([store_cols(qa_ref, j) for j in range(N_PAIR_A)] + [store_split(ka_ref)]
              + [store_pair(qb_ref, j) for j in range(N_PAIR_B)]
              + [store_pair(kb_ref, j) for j in range(N_PAIR_B)] + [store_split(va_ref)]
              + [store_pair(vb_ref, j) for j in range(N_PAIR_B)])
    chunks = [(c0, min(PROJ_CHUNK, len(stores) - c0)) for c0 in range(0, len(stores), PROJ_CHUNK)]

    def product(chunk):
        c0, n = chunk
        return jnp.dot(hn, w_ref[:, c0 * LANES:(c0 + n) * LANES], preferred_element_type=F32)

    pending = [product(ch) for ch in chunks[:PROJ_DEPTH]]
    for i, (c0, n) in enumerate(chunks):
        p = pending.pop(0)
        nq = max(0, min(n, n_qk - c0))
        if nq:
            y = qk_norm(p[:, :nq * LANES], c0)
        if i + PROJ_DEPTH < len(chunks):
            pending.append(product(chunks[i + PROJ_DEPTH]))
        for t in range(n):
            cols = slice(t * LANES, (t + 1) * LANES)
            stores[c0 + t](rope(y[:, cols]) if t < nq else p[:, cols])


def _split_heads(t):
    lo = _lane_lo(t.shape)
    zero = jnp.zeros_like(t)
    return jnp.where(lo, t, zero).astype(BF16), jnp.where(lo, zero, t).astype(BF16)


def _scores(q, k_lo, k_hi, mask):
    dn = (((1,), (1,)), ((), ()))
    s0 = lax.dot_general(q, k_lo, dn, preferred_element_type=F32)
    s1 = lax.dot_general(q, k_hi, dn, preferred_element_type=F32)
    return jnp.where(mask, s0, -jnp.inf), jnp.where(mask, s1, -jnp.inf)


def _softmax_pair(s0, s1, floor0=None, floor1=None):
    m0 = jnp.max(s0, axis=-1, keepdims=True)
    m1 = jnp.max(s1, axis=-1, keepdims=True)
    if floor0 is not None:
        m0, m1 = jnp.maximum(m0, floor0), jnp.maximum(m1, floor1)
    p0 = jnp.exp2(s0 - m0)
    p1 = jnp.exp2(s1 - m1)
    l0 = jnp.sum(p0, axis=-1, keepdims=True)
    l1 = jnp.sum(p1, axis=-1, keepdims=True)
    return m0, m1, l0, l1, p0.astype(BF16), p1.astype(BF16)


def _weighted_values(p0, p1, v_lo, v_hi):
    return (jnp.dot(p0, v_lo, preferred_element_type=F32)
            + jnp.dot(p1, v_hi, preferred_element_type=F32))


def _band_mask(nq, nk, max_dist):
    qi = lax.broadcasted_iota(jnp.int32, (nq, nk), 0)
    kj = lax.broadcasted_iota(jnp.int32, (nq, nk), 1)
    dist = BAND + qi - kj
    return (dist >= 0) & (dist <= max_dist)


def _causal_mask(n):
    qi = lax.broadcasted_iota(jnp.int32, (n, n), 0)
    kj = lax.broadcasted_iota(jnp.int32, (n, n), 1)
    return kj <= qi


def _run_blocks(blocks, depth):
    def scores(block):
        load_q, load_k, _, mask = block[:4]
        return _scores(load_q().astype(BF16), *load_k(), mask)

    pending = [scores(b) for b in blocks[:depth]]
    for j, (_, _, load_v, _, floors, emit) in enumerate(blocks):
        m0, m1, l0, l1, p0, p1 = _softmax_pair(*pending.pop(0), *floors)
        acc = _weighted_values(p0, p1, *load_v())
        if j + depth < len(blocks):
            pending.append(scores(blocks[j + depth]))
        emit(m0, m1, l0, l1, acc)


def _attn_a_kernel(sink_ref, q_ref, k_ref, v_ref, o_ref):
    seq = q_ref.shape[1]
    lo = _lane_lo((BAND, LANES))
    band = _band_mask(BAND, 2 * BAND, WINDOW_A - 1)
    causal = _causal_mask(BAND)
    sinks = [sink_ref[h] * LOG2E for h in HEAD_ORDER_A]

    blocks = []
    for b in range(seq // BAND):
        q_rows = pl.ds(b * BAND, BAND)
        k_rows = q_rows if b == 0 else pl.ds((b - 1) * BAND, 2 * BAND)
        for p in range(N_PAIR_A):
            cols = slice(p * LANES, (p + 1) * LANES)
            sink0, sink1 = sinks[2 * p], sinks[2 * p + 1]

            def emit(m0, m1, l0, l1, acc, q_rows=q_rows, cols=cols, sink0=sink0, sink1=sink1):
                den = jnp.where(lo, l0, l1) + jnp.exp2(jnp.where(lo, sink0 - m0, sink1 - m1))
                o_ref[0, q_rows, cols] = (acc / den).astype(o_ref.dtype)

            blocks.append((lambda q_rows=q_rows, cols=cols: q_ref[0, q_rows, cols],
                           lambda k_rows=k_rows: (k_ref[0, k_rows, :LANES], k_ref[0, k_rows, LANES:]),
                           lambda k_rows=k_rows: (v_ref[0, k_rows, :LANES], v_ref[0, k_rows, LANES:]),
                           causal if b == 0 else band, (sink0, sink1), emit))
    _run_blocks(blocks, DEPTH_A)


def _attn_b_kernel(q_ref, k_ref, v_ref, o_ref, m2_ref, l2_ref, acc2_ref, m3_ref, l3_ref, acc3_ref):
    seq = q_ref.shape[2]
    lo = _lane_lo((BAND, LANES))
    band = _band_mask(BAND, 2 * BAND, BAND)
    causal = _causal_mask(BAND)

    def block(q_rows, k_rows, mask, emit):
        return (lambda: q_ref[0, 0, q_rows, :], lambda: _split_heads(k_ref[0, 0, k_rows, :]),
                lambda: _split_heads(v_ref[0, 0, k_rows, :]), mask, (), emit)

    def store_to(m_ref, l_ref, acc_ref, rows):
        def emit(m0, m1, l0, l1, acc):
            m_ref[rows, :] = jnp.where(lo, m0, m1)
            l_ref[rows, :] = jnp.where(lo, l0, l1)
            acc_ref[rows, :] = acc
        return emit

    def finish(q_rows):
        def emit(m0, m1, l0, l1, acc1):
            m1t, l1t = jnp.where(lo, m0, m1), jnp.where(lo, l0, l1)
            m2, m3 = m2_ref[q_rows, :], m3_ref[q_rows, :]
            m = jnp.maximum(jnp.maximum(m1t, m2), m3)
            a1, a2, a3 = jnp.exp2(m1t - m), jnp.exp2(m2 - m), jnp.exp2(m3 - m)
            den = l1t * a1 + l2_ref[q_rows, :] * a2 + l3_ref[q_rows, :] * a3
            acc = acc1 * a1 + acc2_ref[q_rows, :] * a2 + acc3_ref[q_rows, :] * a3
            o_ref[0, 0, q_rows, :] = (acc / den).astype(o_ref.dtype)
        return emit

    d2, d3 = DILATIONS[1], DILATIONS[2]

    blocks = []
    for r in range(d3):
        rows = pl.ds(r, seq // d3, stride=d3)
        blocks.append(block(rows, rows, causal, store_to(m3_ref, l3_ref, acc3_ref, rows)))
    for b in range(seq // d2 // BAND):
        for r in range(d2):
            q_rows = pl.ds(b * (BAND * d2) + r, BAND, stride=d2)
            k_rows = q_rows if b == 0 else pl.ds((b - 1) * (BAND * d2) + r, 2 * BAND, stride=d2)
            blocks.append(block(q_rows, k_rows, causal if b == 0 else band,
                                store_to(m2_ref, l2_ref, acc2_ref, q_rows)))
    for b in range(seq // BAND):
        q_rows = pl.ds(b * BAND, BAND)
        k_rows = q_rows if b == 0 else pl.ds((b - 1) * BAND, 2 * BAND)
        blocks.append(block(q_rows, k_rows, causal if b == 0 else band, finish(q_rows)))
    _run_blocks(blocks, DEPTH_B)


def _mlp_kernel(x_ref, mixa_ref, mixb_ref, woa_ref, wob_ref, gain_ref, wup_ref, wdn_ref, o_ref,
                *, ff_chunk):
    mixb = jnp.concatenate([mixb_ref[0, j] for j in range(N_PAIR_B)], axis=-1)
    h = (x_ref[0]
         + jnp.dot(mixa_ref[0], woa_ref[...], preferred_element_type=F32)
         + jnp.dot(mixb, wob_ref[...], preferred_element_type=F32))
    ms = jnp.mean(h * h, axis=-1, keepdims=True)
    hn = (h * lax.rsqrt(ms + EPS) * gain_ref[...]).astype(BF16)
    mlp = None
    d_ff = wup_ref.shape[1]
    for c in range(d_ff // ff_chunk):
        u = jnp.dot(hn, wup_ref[:, c * ff_chunk:(c + 1) * ff_chunk], preferred_element_type=F32)
        a = jnp.square(jnp.maximum(u, 0.0)).astype(BF16)
        d = jnp.dot(a, wdn_ref[c * ff_chunk:(c + 1) * ff_chunk, :], preferred_element_type=F32)
        mlp = d if mlp is None else mlp + d
    o_ref[0] = h + mlp


def _const_spec(shape):
    nd = len(shape)
    return pl.BlockSpec(shape, lambda *_: (0,) * nd, pipeline_mode=pl.Buffered(1))


def _tok_spec(tm, width):
    return pl.BlockSpec((1, tm, width), lambda b, i: (b, i, 0))


def _pair_spec(tm):
    return pl.BlockSpec((1, N_PAIR_B, tm, LANES), lambda b, i: (b, 0, i, 0))


def _params(n_axes):
    return pltpu.CompilerParams(dimension_semantics=("parallel",) * n_axes,
                                vmem_limit_bytes=VMEM_LIMIT)


N_FREQ = ROPE_DIM // 2
ROPE_ROWS = 8 * N_FREQ


def _rope_table(positions):
    batch, seq = positions.shape
    inv_freq = ROPE_THETA ** (-(np.arange(0, ROPE_DIM, 2, dtype=np.float32) / ROPE_DIM))
    table = pl.pallas_call(
        _rope_table_kernel,
        grid=(batch,),
        in_specs=[pl.BlockSpec((1, 1, seq), lambda b: (b, 0, 0)), _const_spec((N_FREQ, 1))],
        out_specs=pl.BlockSpec((1, ROPE_ROWS, seq), lambda b: (b, 0, 0)),
        out_shape=jax.ShapeDtypeStruct((batch, ROPE_ROWS, seq), BF16),
        compiler_params=_params(1),
        name="rope_table",
    )(positions.reshape(batch, 1, seq), jnp.asarray(inv_freq[:, None], F32))
    return jnp.swapaxes(table, 1, 2)


def _rope_select():
    sel = np.zeros((ROPE_ROWS, 2 * LANES), np.float32)
    for lane in range(LANES):
        d = lane % HEAD_DIM
        if d < ROPE_DIM:
            for term in range(3):
                sel[term * 2 * N_FREQ + d % N_FREQ, lane] = 1.0
                sel[term * 2 * N_FREQ + N_FREQ + d % N_FREQ, LANES + lane] = -1.0 if d < N_FREQ else 1.0
        else:
            sel[6 * N_FREQ, lane] = 1.0
    return jnp.asarray(sel, BF16)


def _project(h, rope_table, gain, w_in, qk_gain, tm):
    batch, seq, d_model = h.shape
    pair_shape = jax.ShapeDtypeStruct((batch, N_PAIR_B, seq, LANES), F32)
    head = np.arange(2 * LANES) // HEAD_DIM
    head_sum = jnp.asarray((head[:, None] == head[None, :]) / HEAD_DIM, BF16)
    rope_sel = _rope_select()
    return pl.pallas_call(
        _proj_kernel,
        grid=(batch, seq // tm),
        in_specs=[_tok_spec(tm, d_model), _tok_spec(tm, ROPE_ROWS), _const_spec((1, d_model)),
                  _const_spec(w_in.shape), _const_spec((1, QK_WIDTH)),
                  _const_spec(rope_sel.shape), _const_spec(head_sum.shape)],
        out_specs=[_tok_spec(tm, WQA), _tok_spec(tm, 2 * WKA), _tok_spec(tm, 2 * WKA),
                   _pair_spec(tm), _pair_spec(tm), _pair_spec(tm)],
        out_shape=[jax.ShapeDtypeStruct((batch, seq, WQA), BF16),
                   jax.ShapeDtypeStruct((batch, seq, 2 * WKA), BF16),
                   jax.ShapeDtypeStruct((batch, seq, 2 * WKA), BF16),
                   pair_shape, pair_shape, pair_shape],
        compiler_params=_params(2),
        name="proj_qknorm_rope",
    )(h, rope_table, gain, w_in, qk_gain, rope_sel, head_sum)


def _mixer_a(sinks, qa, ka, va):
    batch, seq, _ = qa.shape
    seq_spec = lambda w: pl.BlockSpec((1, seq, w), lambda b: (b, 0, 0))
    return pl.pallas_call(
        _attn_a_kernel,
        grid=(batch,),
        in_specs=[pl.BlockSpec(memory_space=pltpu.SMEM), seq_spec(WQA), seq_spec(2 * WKA),
                  seq_spec(2 * WKA)],
        out_specs=seq_spec(WQA),
        out_shape=jax.ShapeDtypeStruct((batch, seq, WQA), BF16),
        compiler_params=_params(1),
        name="swa_gqa_sinks",
    )(sinks, qa, ka, va)


def _mixer_b(qb, kb, vb):
    batch, _, seq, _ = qb.shape
    head_spec = pl.BlockSpec((1, 1, seq, LANES), lambda b, p: (b, p, 0, 0))
    return pl.pallas_call(
        _attn_b_kernel,
        grid=(batch, N_PAIR_B),
        in_specs=[head_spec, head_spec, head_spec],
        out_specs=head_spec,
        out_shape=jax.ShapeDtypeStruct(qb.shape, BF16),
        scratch_shapes=[pltpu.VMEM((seq, LANES), F32)] * 6,
        compiler_params=_params(2),
        name="dilated_mixture",
    )(qb, kb, vb)


def _out_mlp(h, mix_a, mix_b, w_out_a, w_out_b, gain, w_up, w_dn, tm):
    batch, seq, d_model = h.shape
    return pl.pallas_call(
        functools.partial(_mlp_kernel, ff_chunk=1024),
        grid=(batch, seq // tm),
        in_specs=[_tok_spec(tm, d_model), _tok_spec(tm, WQA), _pair_spec(tm),
                  _const_spec(w_out_a.shape), _const_spec(w_out_b.shape),
                  _const_spec((1, d_model)), _const_spec(w_up.shape), _const_spec(w_dn.shape)],
        out_specs=_tok_spec(tm, d_model),
        out_shape=jax.ShapeDtypeStruct(h.shape, h.dtype),
        compiler_params=_params(2),
        name="outproj_mlp",
    )(h, mix_a, mix_b, w_out_a, w_out_b, gain, w_up, w_dn)


def kernel(x, positions, attn_norm_gain, w_in, q_norm_a, k_norm_a, sinks_a, q_norm_b, k_norm_b,
           w_out, mlp_norm_gain, w_up, w_down):
    batch, seq, d_model = x.shape
    assert seq // DILATIONS[2] == BAND
    tm = 512
    assert seq % tm == 0

    rope_table = _rope_table(positions)

    o_ka, o_va = WQA, WQA + WKA
    o_qb, o_vb = WQA + 2 * WKA, WQA + 2 * WKA + 2 * WB
    qa_cols = np.concatenate([np.arange(h * HEAD_DIM, (h + 1) * HEAD_DIM) for h in HEAD_ORDER_A])
    col_order = np.concatenate([qa_cols, np.arange(o_ka, o_va), np.arange(o_qb, o_vb),
                                np.arange(o_va, o_qb), np.arange(o_vb, o_vb + WB)])
    scale = HEAD_DIM ** -0.5 * LOG2E

    h = x
    for layer in range(w_in.shape[0]):
        qk_gain = jnp.concatenate([
            jnp.tile(q_norm_a[layer] * scale, N_HEADS_A), jnp.tile(k_norm_a[layer], N_KV_A),
            jnp.tile(q_norm_b[layer] * scale, N_HEADS_B), jnp.tile(k_norm_b[layer], N_HEADS_B)])[None]
        qa, ka, va, qb, kb, vb = _project(
            h, rope_table, attn_norm_gain[layer][None], w_in[layer][:, col_order].astype(BF16),
            qk_gain, tm)
        mix_a = _mixer_a(sinks_a[layer].astype(F32), qa, ka, va)
        mix_b = _mixer_b(qb, kb, vb)
        h = _out_mlp(h, mix_a, mix_b, w_out[layer][qa_cols].astype(BF16),
                     w_out[layer][WQA:].astype(BF16), mlp_norm_gain[layer][None],
                     w_up[layer].astype(BF16), w_down[layer].astype(BF16), tm)
    return h
```

```python
import functools
import math

import numpy as np
import jax
import jax.numpy as jnp
from jax import lax
from jax.experimental import pallas as pl
from jax.experimental.pallas import tpu as pltpu

F32 = jnp.float32
BF16 = jnp.bfloat16

LANES = 128
HEAD_DIM = 64
PAIR = 2 * HEAD_DIM
N_HEADS_A = 8
N_KV_A = 2
N_HEADS_B = 8
WINDOW_A = 128
DILATIONS = (1, 4, 16)
BAND = 128
ROPE_DIM = HEAD_DIM // 4
ROPE_THETA = 500000.0
EPS = 1e-6
LOG2E = 1.4426950408889634
TM_PROJ = 1024
TM_MLP = 512
PROJ_CHUNK = 2
PROJ_DEPTH = 2
DEPTH_A = 5
DEPTH_B = 4
VMEM_LIMIT = 56 * 1024 * 1024

WQA = N_HEADS_A * HEAD_DIM
WKA = N_KV_A * HEAD_DIM
WB = N_HEADS_B * HEAD_DIM
QK_WIDTH = WQA + WKA + 2 * WB
N_PAIR_A = WQA // PAIR
N_PAIR_B = WB // PAIR
HEAD_ORDER_A = tuple(h for p in range(N_PAIR_A) for h in (p, p + N_HEADS_A // N_KV_A))


def _lane_lo(shape):
    return lax.broadcasted_iota(jnp.int32, shape, len(shape) - 1) < HEAD_DIM


def _rope_table_kernel(pos_ref, freq_ref, o_ref):
    for i in range(pos_ref.shape[0]):
        ang = freq_ref[...] * pos_ref[i].astype(F32)
        cs = jnp.concatenate([jnp.cos(ang), jnp.sin(ang)], axis=0)
        high = cs.astype(BF16)
        rest = cs - high.astype(F32)
        middle = rest.astype(BF16)
        low = (rest - middle.astype(F32)).astype(BF16)
        row = lax.broadcasted_iota(jnp.int32, cs.shape, 0)
        ones_row = jnp.where(row == 0, 1.0, 0.0).astype(BF16)
        o_ref[i] = jnp.concatenate([high, middle, low, ones_row], axis=0)


def _proj_kernel(x_ref, rope_ref, gain_ref, w_ref, qkgain_ref, ropesel_ref, headsum_ref,
                 qa_ref, ka_ref, va_ref, qb_ref, kb_ref, vb_ref):
    x = x_ref[0]
    ms = jnp.mean(x * x, axis=-1, keepdims=True)
    hn = (x * lax.rsqrt(ms + EPS) * gain_ref[...]).astype(BF16)

    rope = lax.dot_general(rope_ref[0], ropesel_ref[...], (((0,), (0,)), ((), ())),
                           preferred_element_type=F32)
    cos, sin = rope[:, :LANES], rope[:, LANES:]
    tm = x.shape[0]
    lane = lax.broadcasted_iota(jnp.int32, (tm, LANES), 1)
    lo = lane < HEAD_DIM
    first_half = (lane & (HEAD_DIM - 1)) < (ROPE_DIM // 2)

    def qk_norm(p, c0):
        width = p.shape[1]
        mean_sq = jnp.dot((p * p).astype(BF16), headsum_ref[:width, :width],
                          preferred_element_type=F32)
        return p * lax.rsqrt(mean_sq + EPS) * qkgain_ref[:, c0 * LANES:c0 * LANES + width]

    def rope(y):
        partner = jnp.where(first_half,
                            pltpu.roll(y, LANES - ROPE_DIM // 2, 1),
                            pltpu.roll(y, ROPE_DIM // 2, 1))
        return y * cos + partner * sin

    def store_cols(ref, j):
        def store(t):
            ref[0, :, j * LANES:(j + 1) * LANES] = t.astype(ref.dtype)
        return store

    def store_pair(ref, j):
        def store(t):
            ref[0, j] = t
        return store

    def store_split(ref):
        def store(t):
            zero = jnp.zeros_like(t)
            ref[0, :, :LANES] = jnp.where(lo, t, zero).astype(ref.dtype)
            ref[0, :, LANES:] = jnp.where(lo, zero, t).astype(ref.dtype)
        return store

    n_qk = QK_WIDTH // LANES
    stores = ([store_cols(qa_ref, j) for j in range(N_PAIR_A)] + [store_split(ka_ref)]
              + [store_pair(qb_ref, j) for j in range(N_PAIR_B)]
              + [store_pair(kb_ref, j) for j in range(N_PAIR_B)] + [store_split(va_ref)]
              + [store_pair(vb_ref, j) for j in range(N_PAIR_B)])
    chunks = [(c0, min(PROJ_CHUNK, len(stores) - c0)) for c0 in range(0, len(stores), PROJ_CHUNK)]

    def product(chunk):
        c0, n = chunk
        return jnp.dot(hn, w_ref[:, c0 * LANES:(c0 + n) * LANES], preferred_element_type=F32)

    pending = [product(ch) for ch in chunks[:PROJ_DEPTH]]
    for i, (c0, n) in enumerate(chunks):
        p = pending.pop(0)
        nq = max(0, min(n, n_qk - c0))
        if nq:
            y = qk_norm(p[:, :nq * LANES], c0)
        if i + PROJ_DEPTH < len(chunks):
            pending.append(product(chunks[i + PROJ_DEPTH]))
        for t in range(n):
            cols = slice(t * LANES, (t + 1) * LANES)
            stores[c0 + t](rope(y[:, cols]) if t < nq else p[:, cols])


def _split_heads(t):
    lo = _lane_lo(t.shape)
    zero = jnp.zeros_like(t)
    return jnp.where(lo, t, zero).astype(BF16), jnp.where(lo, zero, t).astype(BF16)


def _scores(q, k_lo, k_hi, mask):
    dn = (((1,), (1,)), ((), ()))
    s0 = lax.dot_general(q, k_lo, dn, preferred_element_type=F32)
    s1 = lax.dot_general(q, k_hi, dn, preferred_element_type=F32)
    return jnp.where(mask, s0, -jnp.inf), jnp.where(mask, s1, -jnp.inf)


def _softmax_pair(s0, s1, floor0=None, floor1=None):
    m0 = jnp.max(s0, axis=-1, keepdims=True)
    m1 = jnp.max(s1, axis=-1, keepdims=True)
    if floor0 is not None:
        m0, m1 = jnp.maximum(m0, floor0), jnp.maximum(m1, floor1)
    p0 = jnp.exp2(s0 - m0)
    p1 = jnp.exp2(s1 - m1)
    l0 = jnp.sum(p0, axis=-1, keepdims=True)
    l1 = jnp.sum(p1, axis=-1, keepdims=True)
    return m0, m1, l0, l1, p0.astype(BF16), p1.astype(BF16)


def _weighted_values(p0, p1, v_lo, v_hi):
    return (jnp.dot(p0, v_lo, preferred_element_type=F32)
            + jnp.dot(p1, v_hi, preferred_element_type=F32))


def _band_mask(nq, nk, max_dist):
    qi = lax.broadcasted_iota(jnp.int32, (nq, nk), 0)
    kj = lax.broadcasted_iota(jnp.int32, (nq, nk), 1)
    dist = BAND + qi - kj
    return (dist >= 0) & (dist <= max_dist)


def _causal_mask(n):
    qi = lax.broadcasted_iota(jnp.int32, (n, n), 0)
    kj = lax.broadcasted_iota(jnp.int32, (n, n), 1)
    return kj <= qi


def _run_blocks(blocks, depth):
    def scores(block):
        load_q, load_k, _, mask = block[:4]
        return _scores(load_q().astype(BF16), *load_k(), mask)

    pending = [scores(b) for b in blocks[:depth]]
    for j, (_, _, load_v, _, floors, emit) in enumerate(blocks):
        m0, m1, l0, l1, p0, p1 = _softmax_pair(*pending.pop(0), *floors)
        acc = _weighted_values(p0, p1, *load_v())
        if j + depth < len(blocks):
            pending.append(scores(blocks[j + depth]))
        emit(m0, m1, l0, l1, acc)


def _attn_a_kernel(sink_ref, q_ref, k_ref, v_ref, o_ref):
    seq = q_ref.shape[1]
    lo = _lane_lo((BAND, LANES))
    band = _band_mask(BAND, 2 * BAND, WINDOW_A - 1)
    causal = _causal_mask(BAND)
    sinks = [sink_ref[h] * LOG2E for h in HEAD_ORDER_A]

    blocks = []
    for b in range(seq // BAND):
        q_rows = pl.ds(b * BAND, BAND)
        k_rows = q_rows if b == 0 else pl.ds((b - 1) * BAND, 2 * BAND)
        for p in range(N_PAIR_A):
            cols = slice(p * LANES, (p + 1) * LANES)
            sink0, sink1 = sinks[2 * p], sinks[2 * p + 1]

            def emit(m0, m1, l0, l1, acc, q_rows=q_rows, cols=cols, sink0=sink0, sink1=sink1):
                den = jnp.where(lo, l0, l1) + jnp.exp2(jnp.where(lo, sink0 - m0, sink1 - m1))
                o_ref[0, q_rows, cols] = (acc / den).astype(o_ref.dtype)

            blocks.append((lambda q_rows=q_rows, cols=cols: q_ref[0, q_rows, cols],
                           lambda k_rows=k_rows: (k_ref[0, k_rows, :LANES], k_ref[0, k_rows, LANES:]),
                           lambda k_rows=k_rows: (v_ref[0, k_rows, :LANES], v_ref[0, k_rows, LANES:]),
                           causal if b == 0 else band, (sink0, sink1), emit))
    _run_blocks(blocks, DEPTH_A)


def _attn_b_kernel(q_ref, k_ref, v_ref, o_ref, m2_ref, l2_ref, acc2_ref, m3_ref, l3_ref, acc3_ref):
    seq = q_ref.shape[2]
    lo = _lane_lo((BAND, LANES))
    band = _band_mask(BAND, 2 * BAND, BAND)
    causal = _causal_mask(BAND)

    def block(q_rows, k_rows, mask, emit):
        return (lambda: q_ref[0, 0, q_rows, :], lambda: _split_heads(k_ref[0, 0, k_rows, :]),
                lambda: _split_heads(v_ref[0, 0, k_rows, :]), mask, (), emit)

    def store_to(m_ref, l_ref, acc_ref, rows):
        def emit(m0, m1, l0, l1, acc):
            m_ref[rows, :] = jnp.where(lo, m0, m1)
            l_ref[rows, :] = jnp.where(lo, l0, l1)
            acc_ref[rows, :] = acc
        return emit

    def finish(q_rows):
        def emit(m0, m1, l0, l1, acc1):
            m1t, l1t = jnp.where(lo, m0, m1), jnp.where(lo, l0, l1)
            m2, m3 = m2_ref[q_rows, :], m3_ref[q_rows, :]
            m = jnp.maximum(jnp.maximum(m1t, m2), m3)
            a1, a2, a3 = jnp.exp2(m1t - m), jnp.exp2(m2 - m), jnp.exp2(m3 - m)
            den = l1t * a1 + l2_ref[q_rows, :] * a2 + l3_ref[q_rows, :] * a3
            acc = acc1 * a1 + acc2_ref[q_rows, :] * a2 + acc3_ref[q_rows, :] * a3
            o_ref[0, 0, q_rows, :] = (acc / den).astype(o_ref.dtype)
        return emit

    d2, d3 = DILATIONS[1], DILATIONS[2]

    blocks = []
    for r in range(d3):
        rows = pl.ds(r, seq // d3, stride=d3)
        blocks.append(block(rows, rows, causal, store_to(m3_ref, l3_ref, acc3_ref, rows)))
    for b in range(seq // d2 // BAND):
        for r in range(d2):
            q_rows = pl.ds(b * (BAND * d2) + r, BAND, stride=d2)
            k_rows = q_rows if b == 0 else pl.ds((b - 1) * (BAND * d2) + r, 2 * BAND, stride=d2)
            blocks.append(block(q_rows, k_rows, causal if b == 0 else band,
                                store_to(m2_ref, l2_ref, acc2_ref, q_rows)))
    for b in range(seq // BAND):
        q_rows = pl.ds(b * BAND, BAND)
        k_rows = q_rows if b == 0 else pl.ds((b - 1) * BAND, 2 * BAND)
        blocks.append(block(q_rows, k_rows, causal if b == 0 else band, finish(q_rows)))
    _run_blocks(blocks, DEPTH_B)


def _mlp_kernel(x_ref, mixa_ref, mixb_ref, woa_ref, wob_ref, gain_ref, wup_ref, wdn_ref, o_ref,
                *, ff_chunk):
    mixb = jnp.concatenate([mixb_ref[0, j] for j in range(N_PAIR_B)], axis=-1)
    h = (x_ref[0]
         + jnp.dot(mixa_ref[0], woa_ref[...], preferred_element_type=F32)
         + jnp.dot(mixb, wob_ref[...], preferred_element_type=F32))
    ms = jnp.mean(h * h, axis=-1, keepdims=True)
    hn = (h * lax.rsqrt(ms + EPS) * gain_ref[...]).astype(BF16)
    mlp = None
    d_ff = wup_ref.shape[1]
    for c in range(d_ff // ff_chunk):
        u = jnp.dot(hn, wup_ref[:, c * ff_chunk:(c + 1) * ff_chunk], preferred_element_type=F32)
        a = jnp.square(jnp.maximum(u, 0.0)).astype(BF16)
        d = jnp.dot(a, wdn_ref[c * ff_chunk:(c + 1) * ff_chunk, :], preferred_element_type=F32)
        mlp = d if mlp is None else mlp + d
    o_ref[0] = h + mlp


def _const_spec(shape):
    nd = len(shape)
    return pl.BlockSpec(shape, lambda *_: (0,) * nd, pipeline_mode=pl.Buffered(1))


def _tok_spec(tm, width):
    return pl.BlockSpec((1, tm, width), lambda b, i: (b, i, 0))


def _pair_spec(tm):
    return pl.BlockSpec((1, N_PAIR_B, tm, LANES), lambda b, i: (b, 0, i, 0))


def _params(n_axes):
    return pltpu.CompilerParams(dimension_semantics=("parallel",) * n_axes,
                                vmem_limit_bytes=VMEM_LIMIT)


N_FREQ = ROPE_DIM // 2
ROPE_ROWS = 8 * N_FREQ
ROPE_BATCH = 8


def _rope_table(positions):
    batch, seq = positions.shape
    rows = math.gcd(batch, ROPE_BATCH)
    inv_freq = ROPE_THETA ** (-(np.arange(0, ROPE_DIM, 2, dtype=np.float32) / ROPE_DIM))
    return pl.pallas_call(
        _rope_table_kernel,
        grid=(batch // rows,),
        in_specs=[pl.BlockSpec((rows, 1, seq), lambda b: (b, 0, 0)), _const_spec((N_FREQ, 1))],
        out_specs=pl.BlockSpec((rows, ROPE_ROWS, seq), lambda b: (b, 0, 0)),
        out_shape=jax.ShapeDtypeStruct((batch, ROPE_ROWS, seq), BF16),
        compiler_params=_params(1),
        name="rope_table",
    )(positions.reshape(batch, 1, seq), jnp.asarray(inv_freq[:, None], F32))


def _rope_select():
    sel = np.zeros((ROPE_ROWS, 2 * LANES), np.float32)
    for lane in range(LANES):
        d = lane % HEAD_DIM
        if d < ROPE_DIM:
            for term in range(3):
                sel[term * 2 * N_FREQ + d % N_FREQ, lane] = 1.0
                sel[term * 2 * N_FREQ + N_FREQ + d % N_FREQ, LANES + lane] = -1.0 if d < N_FREQ else 1.0
        else:
            sel[6 * N_FREQ, lane] = 1.0
    return jnp.asarray(sel, BF16)


def _project(h, rope_table, gain, w_in, qk_gain, tm):
    batch, seq, d_model = h.shape
    pair_shape = jax.ShapeDtypeStruct((batch, N_PAIR_B, seq, LANES), F32)
    head = np.arange(2 * LANES) // HEAD_DIM
    head_sum = jnp.asarray((head[:, None] == head[None, :]) / HEAD_DIM, BF16)
    rope_sel = _rope_select()
    return pl.pallas_call(
        _proj_kernel,
        grid=(batch, seq // tm),
        in_specs=[_tok_spec(tm, d_model), pl.BlockSpec((1, ROPE_ROWS, tm), lambda b, i: (b, 0, i)),
                  _const_spec((1, d_model)),
                  _const_spec(w_in.shape), _const_spec((1, QK_WIDTH)),
                  _const_spec(rope_sel.shape), _const_spec(head_sum.shape)],
        out_specs=[_tok_spec(tm, WQA), _tok_spec(tm, 2 * WKA), _tok_spec(tm, 2 * WKA),
                   _pair_spec(tm), _pair_spec(tm), _pair_spec(tm)],
        out_shape=[jax.ShapeDtypeStruct((batch, seq, WQA), BF16),
                   jax.ShapeDtypeStruct((batch, seq, 2 * WKA), BF16),
                   jax.ShapeDtypeStruct((batch, seq, 2 * WKA), BF16),
                   pair_shape, pair_shape, pair_shape],
        compiler_params=_params(2),
        name="proj_qknorm_rope",
    )(h, rope_table, gain, w_in, qk_gain, rope_sel, head_sum)


def _mixer_a(sinks, qa, ka, va):
    batch, seq, _ = qa.shape
    seq_spec = lambda w: pl.BlockSpec((1, seq, w), lambda b: (b, 0, 0))
    return pl.pallas_call(
        _attn_a_kernel,
        grid=(batch,),
        in_specs=[pl.BlockSpec(memory_space=pltpu.SMEM), seq_spec(WQA), seq_spec(2 * WKA),
                  seq_spec(2 * WKA)],
        out_specs=seq_spec(WQA),
        out_shape=jax.ShapeDtypeStruct((batch, seq, WQA), BF16),
        compiler_params=_params(1),
        name="swa_gqa_sinks",
    )(sinks, qa, ka, va)


def _mixer_b(qb, kb, vb):
    batch, _, seq, _ = qb.shape
    head_spec = pl.BlockSpec((1, 1, seq, LANES), lambda b, p: (b, p, 0, 0))
    return pl.pallas_call(
        _attn_b_kernel,
        grid=(batch, N_PAIR_B),
        in_specs=[head_spec, head_spec, head_spec],
        out_specs=head_spec,
        out_shape=jax.ShapeDtypeStruct(qb.shape, BF16),
        scratch_shapes=[pltpu.VMEM((seq, LANES), F32)] * 6,
        compiler_params=_params(2),
        name="dilated_mixture",
    )(qb, kb, vb)


def _out_mlp(h, mix_a, mix_b, w_out_a, w_out_b, gain, w_up, w_dn, tm):
    batch, seq, d_model = h.shape
    return pl.pallas_call(
        functools.partial(_mlp_kernel, ff_chunk=1024),
        grid=(batch, seq // tm),
        in_specs=[_tok_spec(tm, d_model), _tok_spec(tm, WQA), _pair_spec(tm),
                  _const_spec(w_out_a.shape), _const_spec(w_out_b.shape),
                  _const_spec((1, d_model)), _const_spec(w_up.shape), _const_spec(w_dn.shape)],
        out_specs=_tok_spec(tm, d_model),
        out_shape=jax.ShapeDtypeStruct(h.shape, h.dtype),
        compiler_params=_params(2),
        name="outproj_mlp",
    )(h, mix_a, mix_b, w_out_a, w_out_b, gain, w_up, w_dn)


def kernel(x, positions, attn_norm_gain, w_in, q_norm_a, k_norm_a, sinks_a, q_norm_b, k_norm_b,
           w_out, mlp_norm_gain, w_up, w_down):
    batch, seq, d_model = x.shape
    assert seq // DILATIONS[2] == BAND
    assert seq % TM_PROJ == 0 and seq % TM_MLP == 0

    rope_table = _rope_table(positions)

    o_ka, o_va = WQA, WQA + WKA
    o_qb, o_vb = WQA + 2 * WKA, WQA + 2 * WKA + 2 * WB
    qa_cols = np.concatenate([np.arange(h * HEAD_DIM, (h + 1) * HEAD_DIM) for h in HEAD_ORDER_A])
    col_order = np.concatenate([qa_cols, np.arange(o_ka, o_va), np.arange(o_qb, o_vb),
                                np.arange(o_va, o_qb), np.arange(o_vb, o_vb + WB)])
    scale = HEAD_DIM ** -0.5 * LOG2E

    h = x
    for layer in range(w_in.shape[0]):
        qk_gain = jnp.concatenate([
            jnp.tile(q_norm_a[layer] * scale, N_HEADS_A), jnp.tile(k_norm_a[layer], N_KV_A),
            jnp.tile(q_norm_b[layer] * scale, N_HEADS_B), jnp.tile(k_norm_b[layer], N_HEADS_B)])[None]
        qa, ka, va, qb, kb, vb = _project(
            h, rope_table, attn_norm_gain[layer][None], w_in[layer][:, col_order].astype(BF16),
            qk_gain, TM_PROJ)
        mix_a = _mixer_a(sinks_a[layer].astype(F32), qa, ka, va)
        mix_b = _mixer_b(qb, kb, vb)
        h = _out_mlp(h, mix_a, mix_b, w_out[layer][qa_cols].astype(BF16),
                     w_out[layer][WQA:].astype(BF16), mlp_norm_gain[layer][None],
                     w_up[layer].astype(BF16), w_down[layer].astype(BF16), TM_MLP)
    return h
```

```python
import functools
import math

import numpy as np
import jax
import jax.numpy as jnp
from jax import lax
from jax.experimental import pallas as pl
from jax.experimental.pallas import tpu as pltpu

F32 = jnp.float32
BF16 = jnp.bfloat16

LANES = 128
HEAD_DIM = 64
PAIR = 2 * HEAD_DIM
N_HEADS_A = 8
N_KV_A = 2
N_HEADS_B = 8
WINDOW_A = 128
DILATIONS = (1, 4, 16)
BAND = 128
ROPE_DIM = HEAD_DIM // 4
ROPE_THETA = 500000.0
EPS = 1e-6
LOG2E = 1.4426950408889634
TM_PROJ = 1024
TM_MLP = 512
PROJ_CHUNK = 2
PROJ_DEPTH = 2
DEPTH_A = 5
DEPTH_B = 4
VMEM_LIMIT = 56 * 1024 * 1024

WQA = N_HEADS_A * HEAD_DIM
WKA = N_KV_A * HEAD_DIM
WB = N_HEADS_B * HEAD_DIM
QK_WIDTH = WQA + WKA + 2 * WB
N_PAIR_A = WQA // PAIR
N_PAIR_B = WB // PAIR
HEAD_ORDER_A = tuple(h for p in range(N_PAIR_A) for h in (p, p + N_HEADS_A // N_KV_A))


def _lane_lo(shape):
    return lax.broadcasted_iota(jnp.int32, shape, len(shape) - 1) < HEAD_DIM


def _rope_table_kernel(pos_ref, freq_ref, o_ref):
    for i in range(pos_ref.shape[0]):
        ang = freq_ref[...] * pos_ref[i].astype(F32)
        cs = jnp.concatenate([jnp.cos(ang), jnp.sin(ang)], axis=0)
        high = cs.astype(BF16)
        rest = cs - high.astype(F32)
        middle = rest.astype(BF16)
        low = (rest - middle.astype(F32)).astype(BF16)
        row = lax.broadcasted_iota(jnp.int32, cs.shape, 0)
        ones_row = jnp.where(row == 0, 1.0, 0.0).astype(BF16)
        o_ref[i] = jnp.concatenate([high, middle, low, ones_row], axis=0)


def _proj_kernel(x_ref, rope_ref, gain_ref, w_ref, qkgain_ref, ropesel_ref, headsum_ref,
                 qa_ref, ka_ref, va_ref, qb_ref, kb_ref, vb_ref, qd_ref, kd_ref, vd_ref, stage_ref):
    x = x_ref[0]
    ms = jnp.mean(x * x, axis=-1, keepdims=True)
    hn = (x * lax.rsqrt(ms + EPS) * gain_ref[...]).astype(BF16)

    rope = lax.dot_general(rope_ref[0], ropesel_ref[...], (((0,), (0,)), ((), ())),
                           preferred_element_type=F32)
    cos, sin = rope[:, :LANES], rope[:, LANES:]
    tm = x.shape[0]
    lane = lax.broadcasted_iota(jnp.int32, (tm, LANES), 1)
    lo = lane < HEAD_DIM
    first_half = (lane & (HEAD_DIM - 1)) < (ROPE_DIM // 2)

    def qk_norm(p, c0):
        width = p.shape[1]
        mean_sq = jnp.dot((p * p).astype(BF16), headsum_ref[:width, :width],
                          preferred_element_type=F32)
        return p * lax.rsqrt(mean_sq + EPS) * qkgain_ref[:, c0 * LANES:c0 * LANES + width]

    def rope(y):
        partner = jnp.where(first_half,
                            pltpu.roll(y, LANES - ROPE_DIM // 2, 1),
                            pltpu.roll(y, ROPE_DIM // 2, 1))
        return y * cos + partner * sin

    def store_cols(ref, j):
        def store(t):
            ref[0, :, j * LANES:(j + 1) * LANES] = t.astype(ref.dtype)
        return store

    def store_pair(ref, dil_ref, j):
        d = DILATIONS[1]
        assert DILATIONS[-1] == d * d
        stage = stage_ref.at[j % stage_ref.shape[0]]

        def store(t):
            ref[0, j] = t
            for r in range(d):
                stage[r] = ref[0, j, pl.ds(r, tm // d, stride=d), :]
            for r in range(d):
                for s in range(d):
                    dil_ref[0, j, r + d * s] = stage[r, pl.ds(s, tm // (d * d), stride=d), :]
        return store

    def store_split(ref):
        def store(t):
            zero = jnp.zeros_like(t)
            ref[0, :, :LANES] = jnp.where(lo, t, zero).astype(ref.dtype)
            ref[0, :, LANES:] = jnp.where(lo, zero, t).astype(ref.dtype)
        return store

    n_qk = QK_WIDTH // LANES
    stores = ([store_cols(qa_ref, j) for j in range(N_PAIR_A)] + [store_split(ka_ref)]
              + [store_pair(qb_ref, qd_ref, j) for j in range(N_PAIR_B)]
              + [store_pair(kb_ref, kd_ref, j) for j in range(N_PAIR_B)] + [store_split(va_ref)]
              + [store_pair(vb_ref, vd_ref, j) for j in range(N_PAIR_B)])
    chunks = [(c0, min(PROJ_CHUNK, len(stores) - c0)) for c0 in range(0, len(stores), PROJ_CHUNK)]

    def product(chunk):
        c0, n = chunk
        return jnp.dot(hn, w_ref[:, c0 * LANES:(c0 + n) * LANES], preferred_element_type=F32)

    pending = [product(ch) for ch in chunks[:PROJ_DEPTH]]
    for i, (c0, n) in enumerate(chunks):
        p = pending.pop(0)
        nq = max(0, min(n, n_qk - c0))
        if nq:
            y = qk_norm(p[:, :nq * LANES], c0)
        if i + PROJ_DEPTH < len(chunks):
            pending.append(product(chunks[i + PROJ_DEPTH]))
        for t in range(n):
            cols = slice(t * LANES, (t + 1) * LANES)
            stores[c0 + t](rope(y[:, cols]) if t < nq else p[:, cols])


def _split_heads(t):
    lo = _lane_lo(t.shape)
    zero = jnp.zeros_like(t)
    return jnp.where(lo, t, zero).astype(BF16), jnp.where(lo, zero, t).astype(BF16)


def _scores(q, k_lo, k_hi, mask):
    dn = (((1,), (1,)), ((), ()))
    s0 = lax.dot_general(q, k_lo, dn, preferred_element_type=F32)
    s1 = lax.dot_general(q, k_hi, dn, preferred_element_type=F32)
    return jnp.where(mask, s0, -jnp.inf), jnp.where(mask, s1, -jnp.inf)


def _softmax_pair(s0, s1, floor0=None, floor1=None):
    m0 = jnp.max(s0, axis=-1, keepdims=True)
    m1 = jnp.max(s1, axis=-1, keepdims=True)
    if floor0 is not None:
        m0, m1 = jnp.maximum(m0, floor0), jnp.maximum(m1, floor1)
    p0 = jnp.exp2(s0 - m0)
    p1 = jnp.exp2(s1 - m1)
    l0 = jnp.sum(p0, axis=-1, keepdims=True)
    l1 = jnp.sum(p1, axis=-1, keepdims=True)
    return m0, m1, l0, l1, p0.astype(BF16), p1.astype(BF16)


def _weighted_values(p0, p1, v_lo, v_hi):
    return (jnp.dot(p0, v_lo, preferred_element_type=F32)
            + jnp.dot(p1, v_hi, preferred_element_type=F32))


def _band_mask(nq, nk, max_dist):
    qi = lax.broadcasted_iota(jnp.int32, (nq, nk), 0)
    kj = lax.broadcasted_iota(jnp.int32, (nq, nk), 1)
    dist = BAND + qi - kj
    return (dist >= 0) & (dist <= max_dist)


def _causal_mask(n):
    qi = lax.broadcasted_iota(jnp.int32, (n, n), 0)
    kj = lax.broadcasted_iota(jnp.int32, (n, n), 1)
    return kj <= qi


def _run_blocks(blocks, depth):
    def scores(block):
        load_q, load_k, _, mask = block[:4]
        return _scores(load_q().astype(BF16), *load_k(), mask)

    pending = [scores(b) for b in blocks[:depth]]
    for j, (_, _, load_v, _, floors, emit) in enumerate(blocks):
        m0, m1, l0, l1, p0, p1 = _softmax_pair(*pending.pop(0), *floors)
        acc = _weighted_values(p0, p1, *load_v())
        if j + depth < len(blocks):
            pending.append(scores(blocks[j + depth]))
        emit(m0, m1, l0, l1, acc)


def _attn_a_kernel(sink_ref, q_ref, k_ref, v_ref, o_ref):
    seq = q_ref.shape[1]
    lo = _lane_lo((BAND, LANES))
    band = _band_mask(BAND, 2 * BAND, WINDOW_A - 1)
    causal = _causal_mask(BAND)
    sinks = [sink_ref[h] * LOG2E for h in HEAD_ORDER_A]

    blocks = []
    for b in range(seq // BAND):
        q_rows = pl.ds(b * BAND, BAND)
        k_rows = q_rows if b == 0 else pl.ds((b - 1) * BAND, 2 * BAND)
        for p in range(N_PAIR_A):
            cols = slice(p * LANES, (p + 1) * LANES)
            sink0, sink1 = sinks[2 * p], sinks[2 * p + 1]

            def emit(m0, m1, l0, l1, acc, q_rows=q_rows, cols=cols, sink0=sink0, sink1=sink1):
                den = jnp.where(lo, l0, l1) + jnp.exp2(jnp.where(lo, sink0 - m0, sink1 - m1))
                o_ref[0, q_rows, cols] = (acc / den).astype(o_ref.dtype)

            blocks.append((lambda q_rows=q_rows, cols=cols: q_ref[0, q_rows, cols],
                           lambda k_rows=k_rows: (k_ref[0, k_rows, :LANES], k_ref[0, k_rows, LANES:]),
                           lambda k_rows=k_rows: (v_ref[0, k_rows, :LANES], v_ref[0, k_rows, LANES:]),
                           causal if b == 0 else band, (sink0, sink1), emit))
    _run_blocks(blocks, DEPTH_A)


def _attn_b_kernel(q_ref, k_ref, v_ref, qd_ref, kd_ref, vd_ref, o_ref,
                   m2_ref, l2_ref, acc2_ref, m3_ref, l3_ref, acc3_ref):
    seq = q_ref.shape[2]
    lo = _lane_lo((BAND, LANES))
    band = _band_mask(BAND, 2 * BAND, BAND)
    causal = _causal_mask(BAND)

    def block(q_rows, k_rows, mask, emit):
        return (lambda: q_ref[0, 0, q_rows, :], lambda: _split_heads(k_ref[0, 0, k_rows, :]),
                lambda: _split_heads(v_ref[0, 0, k_rows, :]), mask, (), emit)

    def dilated_block(r, emit):
        return (lambda: qd_ref[0, 0, r], lambda: _split_heads(kd_ref[0, 0, r]),
                lambda: _split_heads(vd_ref[0, 0, r]), causal, (), emit)

    def store_to(m_ref, l_ref, acc_ref, rows):
        def emit(m0, m1, l0, l1, acc):
            m_ref[rows, :] = jnp.where(lo, m0, m1)
            l_ref[rows, :] = jnp.where(lo, l0, l1)
            acc_ref[rows, :] = acc
        return emit

    def finish(q_rows):
        def emit(m0, m1, l0, l1, acc1):
            m1t, l1t = jnp.where(lo, m0, m1), jnp.where(lo, l0, l1)
            m2, m3 = m2_ref[q_rows, :], m3_ref[q_rows, :]
            m = jnp.maximum(jnp.maximum(m1t, m2), m3)
            a1, a2, a3 = jnp.exp2(m1t - m), jnp.exp2(m2 - m), jnp.exp2(m3 - m)
            den = l1t * a1 + l2_ref[q_rows, :] * a2 + l3_ref[q_rows, :] * a3
            acc = acc1 * a1 + acc2_ref[q_rows, :] * a2 + acc3_ref[q_rows, :] * a3
            o_ref[0, 0, q_rows, :] = (acc / den).astype(o_ref.dtype)
        return emit

    d2, d3 = DILATIONS[1], DILATIONS[2]

    blocks = []
    for r in range(d3):
        rows = pl.ds(r, seq // d3, stride=d3)
        blocks.append(dilated_block(r, store_to(m3_ref, l3_ref, acc3_ref, rows)))
    for b in range(seq // d2 // BAND):
        for r in range(d2):
            q_rows = pl.ds(b * (BAND * d2) + r, BAND, stride=d2)
            k_rows = q_rows if b == 0 else pl.ds((b - 1) * (BAND * d2) + r, 2 * BAND, stride=d2)
            blocks.append(block(q_rows, k_rows, causal if b == 0 else band,
                                store_to(m2_ref, l2_ref, acc2_ref, q_rows)))
    for b in range(seq // BAND):
        q_rows = pl.ds(b * BAND, BAND)
        k_rows = q_rows if b == 0 else pl.ds((b - 1) * BAND, 2 * BAND)
        blocks.append(block(q_rows, k_rows, causal if b == 0 else band, finish(q_rows)))
    _run_blocks(blocks, DEPTH_B)


def _mlp_kernel(x_ref, mixa_ref, mixb_ref, woa_ref, wob_ref, gain_ref, wup_ref, wdn_ref, o_ref,
                *, ff_chunk):
    mixb = jnp.concatenate([mixb_ref[0, j] for j in range(N_PAIR_B)], axis=-1)
    h = (x_ref[0]
         + jnp.dot(mixa_ref[0], woa_ref[...], preferred_element_type=F32)
         + jnp.dot(mixb, wob_ref[...], preferred_element_type=F32))
    ms = jnp.mean(h * h, axis=-1, keepdims=True)
    hn = (h * lax.rsqrt(ms + EPS) * gain_ref[...]).astype(BF16)
    mlp = None
    d_ff = wup_ref.shape[1]
    for c in range(d_ff // ff_chunk):
        u = jnp.dot(hn, wup_ref[:, c * ff_chunk:(c + 1) * ff_chunk], preferred_element_type=F32)
        a = jnp.square(jnp.maximum(u, 0.0)).astype(BF16)
        d = jnp.dot(a, wdn_ref[c * ff_chunk:(c + 1) * ff_chunk, :], preferred_element_type=F32)
        mlp = d if mlp is None else mlp + d
    o_ref[0] = h + mlp


def _const_spec(shape):
    nd = len(shape)
    return pl.BlockSpec(shape, lambda *_: (0,) * nd, pipeline_mode=pl.Buffered(1))


def _tok_spec(tm, width):
    return pl.BlockSpec((1, tm, width), lambda b, i: (b, i, 0))


def _pair_spec(tm):
    return pl.BlockSpec((1, N_PAIR_B, tm, LANES), lambda b, i: (b, 0, i, 0))


def _params(n_axes):
    return pltpu.CompilerParams(dimension_semantics=("parallel",) * n_axes,
                                vmem_limit_bytes=VMEM_LIMIT)


N_FREQ = ROPE_DIM // 2
ROPE_ROWS = 8 * N_FREQ
ROPE_BATCH = 8


def _rope_table(positions):
    batch, seq = positions.shape
    rows = math.gcd(batch, ROPE_BATCH)
    inv_freq = ROPE_THETA ** (-(np.arange(0, ROPE_DIM, 2, dtype=np.float32) / ROPE_DIM))
    return pl.pallas_call(
        _rope_table_kernel,
        grid=(batch // rows,),
        in_specs=[pl.BlockSpec((rows, 1, seq), lambda b: (b, 0, 0)), _const_spec((N_FREQ, 1))],
        out_specs=pl.BlockSpec((rows, ROPE_ROWS, seq), lambda b: (b, 0, 0)),
        out_shape=jax.ShapeDtypeStruct((batch, ROPE_ROWS, seq), BF16),
        compiler_params=_params(1),
        name="rope_table",
    )(positions.reshape(batch, 1, seq), jnp.asarray(inv_freq[:, None], F32))


def _rope_select():
    sel = np.zeros((ROPE_ROWS, 2 * LANES), np.float32)
    for lane in range(LANES):
        d = lane % HEAD_DIM
        if d < ROPE_DIM:
            for term in range(3):
                sel[term * 2 * N_FREQ + d % N_FREQ, lane] = 1.0
                sel[term * 2 * N_FREQ + N_FREQ + d % N_FREQ, LANES + lane] = -1.0 if d < N_FREQ else 1.0
        else:
            sel[6 * N_FREQ, lane] = 1.0
    return jnp.asarray(sel, BF16)


def _project(h, rope_table, gain, w_in, qk_gain, tm):
    batch, seq, d_model = h.shape
    pair_shape = jax.ShapeDtypeStruct((batch, N_PAIR_B, seq, LANES), F32)
    head = np.arange(2 * LANES) // HEAD_DIM
    head_sum = jnp.asarray((head[:, None] == head[None, :]) / HEAD_DIM, BF16)
    rope_sel = _rope_select()
    d = DILATIONS[-1]
    dil_shape = jax.ShapeDtypeStruct((batch, N_PAIR_B, d, seq // d, LANES), F32)
    dil_spec = pl.BlockSpec((1, N_PAIR_B, d, tm // d, LANES), lambda b, i: (b, 0, 0, i, 0))
    return pl.pallas_call(
        _proj_kernel,
        grid=(batch, seq // tm),
        in_specs=[_tok_spec(tm, d_model), pl.BlockSpec((1, ROPE_ROWS, tm), lambda b, i: (b, 0, i)),
                  _const_spec((1, d_model)),
                  _const_spec(w_in.shape), _const_spec((1, QK_WIDTH)),
                  _const_spec(rope_sel.shape), _const_spec(head_sum.shape)],
        out_specs=[_tok_spec(tm, WQA), _tok_spec(tm, 2 * WKA), _tok_spec(tm, 2 * WKA),
                   _pair_spec(tm), _pair_spec(tm), _pair_spec(tm), dil_spec, dil_spec, dil_spec],
        out_shape=[jax.ShapeDtypeStruct((batch, seq, WQA), BF16),
                   jax.ShapeDtypeStruct((batch, seq, 2 * WKA), BF16),
                   jax.ShapeDtypeStruct((batch, seq, 2 * WKA), BF16),
                   pair_shape, pair_shape, pair_shape, dil_shape, dil_shape, dil_shape],
        scratch_shapes=[pltpu.VMEM((N_PAIR_B, DILATIONS[1], tm // DILATIONS[1], LANES), F32)],
        compiler_params=_params(2),
        name="proj_qknorm_rope",
    )(h, rope_table, gain, w_in, qk_gain, rope_sel, head_sum)


def _mixer_a(sinks, qa, ka, va):
    batch, seq, _ = qa.shape
    seq_spec = lambda w: pl.BlockSpec((1, seq, w), lambda b: (b, 0, 0))
    return pl.pallas_call(
        _attn_a_kernel,
        grid=(batch,),
        in_specs=[pl.BlockSpec(memory_space=pltpu.SMEM), seq_spec(WQA), seq_spec(2 * WKA),
                  seq_spec(2 * WKA)],
        out_specs=seq_spec(WQA),
        out_shape=jax.ShapeDtypeStruct((batch, seq, WQA), BF16),
        compiler_params=_params(1),
        name="swa_gqa_sinks",
    )(sinks, qa, ka, va)


def _mixer_b(qb, kb, vb, qd, kd, vd):
    batch, _, seq, _ = qb.shape
    head_spec = pl.BlockSpec((1, 1, seq, LANES), lambda b, p: (b, p, 0, 0))
    dil_spec = pl.BlockSpec((1, 1) + qd.shape[2:], lambda b, p: (b, p, 0, 0, 0))
    return pl.pallas_call(
        _attn_b_kernel,
        grid=(batch, N_PAIR_B),
        in_specs=[head_spec, head_spec, head_spec, dil_spec, dil_spec, dil_spec],
        out_specs=head_spec,
        out_shape=jax.ShapeDtypeStruct(qb.shape, BF16),
        scratch_shapes=[pltpu.VMEM((seq, LANES), F32)] * 6,
        compiler_params=_params(2),
        name="dilated_mixture",
    )(qb, kb, vb, qd, kd, vd)


def _out_mlp(h, mix_a, mix_b, w_out_a, w_out_b, gain, w_up, w_dn, tm):
    batch, seq, d_model = h.shape
    return pl.pallas_call(
        functools.partial(_mlp_kernel, ff_chunk=1024),
        grid=(batch, seq // tm),
        in_specs=[_tok_spec(tm, d_model), _tok_spec(tm, WQA), _pair_spec(tm),
                  _const_spec(w_out_a.shape), _const_spec(w_out_b.shape),
                  _const_spec((1, d_model)), _const_spec(w_up.shape), _const_spec(w_dn.shape)],
        out_specs=_tok_spec(tm, d_model),
        out_shape=jax.ShapeDtypeStruct(h.shape, h.dtype),
        compiler_params=_params(2),
        name="outproj_mlp",
    )(h, mix_a, mix_b, w_out_a, w_out_b, gain, w_up, w_dn)


def kernel(x, positions, attn_norm_gain, w_in, q_norm_a, k_norm_a, sinks_a, q_norm_b, k_norm_b,
           w_out, mlp_norm_gain, w_up, w_down):
    batch, seq, d_model = x.shape
    assert seq // DILATIONS[2] == BAND
    assert seq % TM_PROJ == 0 and seq % TM_MLP == 0

    rope_table = _rope_table(positions)

    o_ka, o_va = WQA, WQA + WKA
    o_qb, o_vb = WQA + 2 * WKA, WQA + 2 * WKA + 2 * WB
    qa_cols = np.concatenate([np.arange(h * HEAD_DIM, (h + 1) * HEAD_DIM) for h in HEAD_ORDER_A])
    col_order = np.concatenate([qa_cols, np.arange(o_ka, o_va), np.arange(o_qb, o_vb),
                                np.arange(o_va, o_qb), np.arange(o_vb, o_vb + WB)])
    scale = HEAD_DIM ** -0.5 * LOG2E

    h = x
    for layer in range(w_in.shape[0]):
        qk_gain = jnp.concatenate([
            jnp.tile(q_norm_a[layer] * scale, N_HEADS_A), jnp.tile(k_norm_a[layer], N_KV_A),
            jnp.tile(q_norm_b[layer] * scale, N_HEADS_B), jnp.tile(k_norm_b[layer], N_HEADS_B)])[None]
        qa, ka, va, qb, kb, vb, qd, kd, vd = _project(
            h, rope_table, attn_norm_gain[layer][None], w_in[layer][:, col_order].astype(BF16),
            qk_gain, TM_PROJ)
        mix_a = _mixer_a(sinks_a[layer].astype(F32), qa, ka, va)
        mix_b = _mixer_b(qb, kb, vb, qd, kd, vd)
        h = _out_mlp(h, mix_a, mix_b, w_out[layer][qa_cols].astype(BF16),
                     w_out[layer][WQA:].astype(BF16), mlp_norm_gain[layer][None],
                     w_up[layer].astype(BF16), w_down[layer].astype(BF16), TM_MLP)
    return h
```

```python
import functools
import math

import numpy as np
import jax
import jax.numpy as jnp
from jax import lax
from jax.experimental import pallas as pl
from jax.experimental.pallas import tpu as pltpu

F32 = jnp.float32
BF16 = jnp.bfloat16

LANES = 128
HEAD_DIM = 64
PAIR = 2 * HEAD_DIM
N_HEADS_A = 8
N_KV_A = 2
N_HEADS_B = 8
WINDOW_A = 128
DILATIONS = (1, 4, 16)
BAND = 128
ROPE_DIM = HEAD_DIM // 4
ROPE_THETA = 500000.0
EPS = 1e-6
LOG2E = 1.4426950408889634
TM_PROJ = 1024
TM_MLP = 512
PROJ_CHUNK = 2
PROJ_DEPTH = 2
DEPTH_A = 5
DEPTH_B = 4
VMEM_LIMIT = 56 * 1024 * 1024

WQA = N_HEADS_A * HEAD_DIM
WKA = N_KV_A * HEAD_DIM
WB = N_HEADS_B * HEAD_DIM
QK_WIDTH = WQA + WKA + 2 * WB
N_PAIR_A = WQA // PAIR
N_PAIR_B = WB // PAIR
HEAD_ORDER_A = tuple(h for p in range(N_PAIR_A) for h in (p, p + N_HEADS_A // N_KV_A))


def _lane_lo(shape):
    return lax.broadcasted_iota(jnp.int32, shape, len(shape) - 1) < HEAD_DIM


def _rope_table_kernel(pos_ref, freq_ref, o_ref):
    for i in range(pos_ref.shape[0]):
        ang = freq_ref[...] * pos_ref[i].astype(F32)
        cs = jnp.concatenate([jnp.cos(ang), jnp.sin(ang)], axis=0)
        high = cs.astype(BF16)
        rest = cs - high.astype(F32)
        middle = rest.astype(BF16)
        low = (rest - middle.astype(F32)).astype(BF16)
        row = lax.broadcasted_iota(jnp.int32, cs.shape, 0)
        ones_row = jnp.where(row == 0, 1.0, 0.0).astype(BF16)
        o_ref[i] = jnp.concatenate([high, middle, low, ones_row], axis=0)


def _proj_kernel(x_ref, rope_ref, gain_ref, w_ref, qkgain_ref, ropesel_ref, headsum_ref,
                 qa_ref, ka_ref, va_ref, qb_ref, kb_ref, vb_ref):
    x = x_ref[0]
    ms = jnp.mean(x * x, axis=-1, keepdims=True)
    hn = (x * lax.rsqrt(ms + EPS) * gain_ref[...]).astype(BF16)

    rope = lax.dot_general(rope_ref[0], ropesel_ref[...], (((0,), (0,)), ((), ())),
                           preferred_element_type=F32)
    cos, sin = rope[:, :LANES], rope[:, LANES:]
    tm = x.shape[0]
    lane = lax.broadcasted_iota(jnp.int32, (tm, LANES), 1)
    lo = lane < HEAD_DIM
    first_half = (lane & (HEAD_DIM - 1)) < (ROPE_DIM // 2)

    def qk_norm(p, c0):
        width = p.shape[1]
        mean_sq = jnp.dot((p * p).astype(BF16), headsum_ref[:width, :width],
                          preferred_element_type=F32)
        return p * lax.rsqrt(mean_sq + EPS) * qkgain_ref[:, c0 * LANES:c0 * LANES + width]

    def rope(y):
        partner = jnp.where(first_half,
                            pltpu.roll(y, LANES - ROPE_DIM // 2, 1),
                            pltpu.roll(y, ROPE_DIM // 2, 1))
        return y * cos + partner * sin

    def store_cols(ref, j):
        def store(t):
            ref[0, :, j * LANES:(j + 1) * LANES] = t.astype(ref.dtype)
        return store

    def store_pair(ref, j):
        def store(t):
            ref[0, j] = t
        return store

    def store_split(ref):
        def store(t):
            zero = jnp.zeros_like(t)
            ref[0, :, :LANES] = jnp.where(lo, t, zero).astype(ref.dtype)
            ref[0, :, LANES:] = jnp.where(lo, zero, t).astype(ref.dtype)
        return store

    n_qk = QK_WIDTH // LANES
    stores = ([store_cols(qa_ref, j) for j in range(N_PAIR_A)] + [store_split(ka_ref)]
              + [store_pair(qb_ref, j) for j in range(N_PAIR_B)]
              + [store_pair(kb_ref, j) for j in range(N_PAIR_B)] + [store_split(va_ref)]
              + [store_pair(vb_ref, j) for j in range(N_PAIR_B)])
    chunks = [(c0, min(PROJ_CHUNK, len(stores) - c0)) for c0 in range(0, len(stores), PROJ_CHUNK)]

    def product(chunk):
        c0, n = chunk
        return jnp.dot(hn, w_ref[:, c0 * LANES:(c0 + n) * LANES], preferred_element_type=F32)

    pending = [product(ch) for ch in chunks[:PROJ_DEPTH]]
    for i, (c0, n) in enumerate(chunks):
        p = pending.pop(0)
        nq = max(0, min(n, n_qk - c0))
        if nq:
            y = qk_norm(p[:, :nq * LANES], c0)
        if i + PROJ_DEPTH < len(chunks):
            pending.append(product(chunks[i + PROJ_DEPTH]))
        for t in range(n):
            cols = slice(t * LANES, (t + 1) * LANES)
            stores[c0 + t](rope(y[:, cols]) if t < nq else p[:, cols])


def _split_heads(t):
    lo = _lane_lo(t.shape)
    zero = jnp.zeros_like(t)
    return jnp.where(lo, t, zero).astype(BF16), jnp.where(lo, zero, t).astype(BF16)


def _scores(q, k_lo, k_hi, mask):
    dn = (((1,), (1,)), ((), ()))
    s0 = lax.dot_general(q, k_lo, dn, preferred_element_type=F32)
    s1 = lax.dot_general(q, k_hi, dn, preferred_element_type=F32)
    return jnp.where(mask, s0, -jnp.inf), jnp.where(mask, s1, -jnp.inf)


def _softmax_pair(s0, s1, floor0=None, floor1=None):
    m0 = jnp.max(s0, axis=-1, keepdims=True)
    m1 = jnp.max(s1, axis=-1, keepdims=True)
    if floor0 is not None:
        m0, m1 = jnp.maximum(m0, floor0), jnp.maximum(m1, floor1)
    p0 = jnp.exp2(s0 - m0)
    p1 = jnp.exp2(s1 - m1)
    l0 = jnp.sum(p0, axis=-1, keepdims=True)
    l1 = jnp.sum(p1, axis=-1, keepdims=True)
    return m0, m1, l0, l1, p0.astype(BF16), p1.astype(BF16)


def _weighted_values(p0, p1, v_lo, v_hi):
    return (jnp.dot(p0, v_lo, preferred_element_type=F32)
            + jnp.dot(p1, v_hi, preferred_element_type=F32))


def _band_mask(nq, nk, max_dist):
    qi = lax.broadcasted_iota(jnp.int32, (nq, nk), 0)
    kj = lax.broadcasted_iota(jnp.int32, (nq, nk), 1)
    dist = BAND + qi - kj
    return (dist >= 0) & (dist <= max_dist)


def _causal_mask(n):
    qi = lax.broadcasted_iota(jnp.int32, (n, n), 0)
    kj = lax.broadcasted_iota(jnp.int32, (n, n), 1)
    return kj <= qi


def _run_blocks(blocks, depth):
    def scores(block):
        load_q, load_k, _, mask = block[:4]
        return _scores(load_q().astype(BF16), *load_k(), mask)

    pending = [scores(b) for b in blocks[:depth]]
    for j, (_, _, load_v, _, floors, emit) in enumerate(blocks):
        m0, m1, l0, l1, p0, p1 = _softmax_pair(*pending.pop(0), *floors)
        acc = _weighted_values(p0, p1, *load_v())
        if j + depth < len(blocks):
            pending.append(scores(blocks[j + depth]))
        emit(m0, m1, l0, l1, acc)


def _attn_a_kernel(sink_ref, q_ref, k_ref, v_ref, o_ref):
    seq = q_ref.shape[1]
    lo = _lane_lo((BAND, LANES))
    band = _band_mask(BAND, 2 * BAND, WINDOW_A - 1)
    causal = _causal_mask(BAND)
    sinks = [sink_ref[h] * LOG2E for h in HEAD_ORDER_A]

    blocks = []
    for b in range(seq // BAND):
        q_rows = pl.ds(b * BAND, BAND)
        k_rows = q_rows if b == 0 else pl.ds((b - 1) * BAND, 2 * BAND)
        for p in range(N_PAIR_A):
            cols = slice(p * LANES, (p + 1) * LANES)
            sink0, sink1 = sinks[2 * p], sinks[2 * p + 1]

            def emit(m0, m1, l0, l1, acc, q_rows=q_rows, cols=cols, sink0=sink0, sink1=sink1):
                den = jnp.where(lo, l0, l1) + jnp.exp2(jnp.where(lo, sink0 - m0, sink1 - m1))
                o_ref[0, q_rows, cols] = (acc / den).astype(o_ref.dtype)

            blocks.append((lambda q_rows=q_rows, cols=cols: q_ref[0, q_rows, cols],
                           lambda k_rows=k_rows: (k_ref[0, k_rows, :LANES], k_ref[0, k_rows, LANES:]),
                           lambda k_rows=k_rows: (v_ref[0, k_rows, :LANES], v_ref[0, k_rows, LANES:]),
                           causal if b == 0 else band, (sink0, sink1), emit))
    _run_blocks(blocks, DEPTH_A)


def _attn_b_kernel(q_ref, k_ref, v_ref, o_ref, m2_ref, l2_ref, acc2_ref, m3_ref, l3_ref, acc3_ref,
                   stage_ref, dil_ref):
    seq = q_ref.shape[2]
    lo = _lane_lo((BAND, LANES))
    band = _band_mask(BAND, 2 * BAND, BAND)
    causal = _causal_mask(BAND)

    def block(q_rows, k_rows, mask, emit):
        return (lambda: q_ref[0, 0, q_rows, :], lambda: _split_heads(k_ref[0, 0, k_rows, :]),
                lambda: _split_heads(v_ref[0, 0, k_rows, :]), mask, (), emit)

    def store_to(m_ref, l_ref, acc_ref, rows):
        def emit(m0, m1, l0, l1, acc):
            m_ref[rows, :] = jnp.where(lo, m0, m1)
            l_ref[rows, :] = jnp.where(lo, l0, l1)
            acc_ref[rows, :] = acc
        return emit

    def finish(q_rows):
        def emit(m0, m1, l0, l1, acc1):
            m1t, l1t = jnp.where(lo, m0, m1), jnp.where(lo, l0, l1)
            m2, m3 = m2_ref[q_rows, :], m3_ref[q_rows, :]
            m = jnp.maximum(jnp.maximum(m1t, m2), m3)
            a1, a2, a3 = jnp.exp2(m1t - m), jnp.exp2(m2 - m), jnp.exp2(m3 - m)
            den = l1t * a1 + l2_ref[q_rows, :] * a2 + l3_ref[q_rows, :] * a3
            acc = acc1 * a1 + acc2_ref[q_rows, :] * a2 + acc3_ref[q_rows, :] * a3
            o_ref[0, 0, q_rows, :] = (acc / den).astype(o_ref.dtype)
        return emit

    d2, d3 = DILATIONS[1], DILATIONS[2]

    for t, ref in enumerate((q_ref, k_ref, v_ref)):
        for r in range(d2):
            stage_ref[t, r] = ref[0, 0, pl.ds(r, seq // d2, stride=d2), :]
        for r in range(d2):
            for s in range(d2):
                dil_ref[t, r + d2 * s] = stage_ref[t, r, pl.ds(s, seq // d3, stride=d2), :]

    blocks = []
    for b in range(seq // d2 // BAND):
        for r in range(d2):
            q_rows = pl.ds(b * (BAND * d2) + r, BAND, stride=d2)
            k_rows = q_rows if b == 0 else pl.ds((b - 1) * (BAND * d2) + r, 2 * BAND, stride=d2)
            blocks.append(block(q_rows, k_rows, causal if b == 0 else band,
                                store_to(m2_ref, l2_ref, acc2_ref, q_rows)))
    for r in range(d3):
        rows = pl.ds(r, seq // d3, stride=d3)
        blocks.append((lambda r=r: dil_ref[0, r], lambda r=r: _split_heads(dil_ref[1, r]),
                       lambda r=r: _split_heads(dil_ref[2, r]), causal, (),
                       store_to(m3_ref, l3_ref, acc3_ref, rows)))
    for b in range(seq // BAND):
        q_rows = pl.ds(b * BAND, BAND)
        k_rows = q_rows if b == 0 else pl.ds((b - 1) * BAND, 2 * BAND)
        blocks.append(block(q_rows, k_rows, causal if b == 0 else band, finish(q_rows)))
    _run_blocks(blocks, DEPTH_B)


def _mlp_kernel(x_ref, mixa_ref, mixb_ref, woa_ref, wob_ref, gain_ref, wup_ref, wdn_ref, o_ref,
                *, ff_chunk):
    mixb = jnp.concatenate([mixb_ref[0, j] for j in range(N_PAIR_B)], axis=-1)
    h = (x_ref[0]
         + jnp.dot(mixa_ref[0], woa_ref[...], preferred_element_type=F32)
         + jnp.dot(mixb, wob_ref[...], preferred_element_type=F32))
    ms = jnp.mean(h * h, axis=-1, keepdims=True)
    hn = (h * lax.rsqrt(ms + EPS) * gain_ref[...]).astype(BF16)
    mlp = None
    d_ff = wup_ref.shape[1]
    for c in range(d_ff // ff_chunk):
        u = jnp.dot(hn, wup_ref[:, c * ff_chunk:(c + 1) * ff_chunk], preferred_element_type=F32)
        a = jnp.square(jnp.maximum(u, 0.0)).astype(BF16)
        d = jnp.dot(a, wdn_ref[c * ff_chunk:(c + 1) * ff_chunk, :], preferred_element_type=F32)
        mlp = d if mlp is None else mlp + d
    o_ref[0] = h + mlp


def _const_spec(shape):
    nd = len(shape)
    return pl.BlockSpec(shape, lambda *_: (0,) * nd, pipeline_mode=pl.Buffered(1))


def _tok_spec(tm, width):
    return pl.BlockSpec((1, tm, width), lambda b, i: (b, i, 0))


def _pair_spec(tm):
    return pl.BlockSpec((1, N_PAIR_B, tm, LANES), lambda b, i: (b, 0, i, 0))


def _params(n_axes):
    return pltpu.CompilerParams(dimension_semantics=("parallel",) * n_axes,
                                vmem_limit_bytes=VMEM_LIMIT)


N_FREQ = ROPE_DIM // 2
ROPE_ROWS = 8 * N_FREQ
ROPE_BATCH = 8


def _rope_table(positions):
    batch, seq = positions.shape
    rows = math.gcd(batch, ROPE_BATCH)
    inv_freq = ROPE_THETA ** (-(np.arange(0, ROPE_DIM, 2, dtype=np.float32) / ROPE_DIM))
    return pl.pallas_call(
        _rope_table_kernel,
        grid=(batch // rows,),
        in_specs=[pl.BlockSpec((rows, 1, seq), lambda b: (b, 0, 0)), _const_spec((N_FREQ, 1))],
        out_specs=pl.BlockSpec((rows, ROPE_ROWS, seq), lambda b: (b, 0, 0)),
        out_shape=jax.ShapeDtypeStruct((batch, ROPE_ROWS, seq), BF16),
        compiler_params=_params(1),
        name="rope_table",
    )(positions.reshape(batch, 1, seq), jnp.asarray(inv_freq[:, None], F32))


def _rope_select():
    sel = np.zeros((ROPE_ROWS, 2 * LANES), np.float32)
    for lane in range(LANES):
        d = lane % HEAD_DIM
        if d < ROPE_DIM:
            for term in range(3):
                sel[term * 2 * N_FREQ + d % N_FREQ, lane] = 1.0
                sel[term * 2 * N_FREQ + N_FREQ + d % N_FREQ, LANES + lane] = -1.0 if d < N_FREQ else 1.0
        else:
            sel[6 * N_FREQ, lane] = 1.0
    return jnp.asarray(sel, BF16)


def _project(h, rope_table, gain, w_in, qk_gain, tm):
    batch, seq, d_model = h.shape
    pair_shape = jax.ShapeDtypeStruct((batch, N_PAIR_B, seq, LANES), F32)
    head = np.arange(2 * LANES) // HEAD_DIM
    head_sum = jnp.asarray((head[:, None] == head[None, :]) / HEAD_DIM, BF16)
    rope_sel = _rope_select()
    return pl.pallas_call(
        _proj_kernel,
        grid=(batch, seq // tm),
        in_specs=[_tok_spec(tm, d_model), pl.BlockSpec((1, ROPE_ROWS, tm), lambda b, i: (b, 0, i)),
                  _const_spec((1, d_model)),
                  _const_spec(w_in.shape), _const_spec((1, QK_WIDTH)),
                  _const_spec(rope_sel.shape), _const_spec(head_sum.shape)],
        out_specs=[_tok_spec(tm, WQA), _tok_spec(tm, 2 * WKA), _tok_spec(tm, 2 * WKA),
                   _pair_spec(tm), _pair_spec(tm), _pair_spec(tm)],
        out_shape=[jax.ShapeDtypeStruct((batch, seq, WQA), BF16),
                   jax.ShapeDtypeStruct((batch, seq, 2 * WKA), BF16),
                   jax.ShapeDtypeStruct((batch, seq, 2 * WKA), BF16),
                   pair_shape, pair_shape, pair_shape],
        compiler_params=_params(2),
        name="proj_qknorm_rope",
    )(h, rope_table, gain, w_in, qk_gain, rope_sel, head_sum)


def _mixer_a(sinks, qa, ka, va):
    batch, seq, _ = qa.shape
    seq_spec = lambda w: pl.BlockSpec((1, seq, w), lambda b: (b, 0, 0))
    return pl.pallas_call(
        _attn_a_kernel,
        grid=(batch,),
        in_specs=[pl.BlockSpec(memory_space=pltpu.SMEM), seq_spec(WQA), seq_spec(2 * WKA),
                  seq_spec(2 * WKA)],
        out_specs=seq_spec(WQA),
        out_shape=jax.ShapeDtypeStruct((batch, seq, WQA), BF16),
        compiler_params=_params(1),
        name="swa_gqa_sinks",
    )(sinks, qa, ka, va)


def _mixer_b(qb, kb, vb):
    batch, _, seq, _ = qb.shape
    head_spec = pl.BlockSpec((1, 1, seq, LANES), lambda b, p: (b, p, 0, 0))
    return pl.pallas_call(
        _attn_b_kernel,
        grid=(batch, N_PAIR_B),
        in_specs=[head_spec, head_spec, head_spec],
        out_specs=head_spec,
        out_shape=jax.ShapeDtypeStruct(qb.shape, BF16),
        scratch_shapes=[pltpu.VMEM((seq, LANES), F32)] * 6
        + [pltpu.VMEM((3, DILATIONS[1], seq // DILATIONS[1], LANES), F32),
           pltpu.VMEM((3, DILATIONS[2], seq // DILATIONS[2], LANES), F32)],
        compiler_params=_params(2),
        name="dilated_mixture",
    )(qb, kb, vb)


def _out_mlp(h, mix_a, mix_b, w_out_a, w_out_b, gain, w_up, w_dn, tm):
    batch, seq, d_model = h.shape
    return pl.pallas_call(
        functools.partial(_mlp_kernel, ff_chunk=1024),
        grid=(batch, seq // tm),
        in_specs=[_tok_spec(tm, d_model), _tok_spec(tm, WQA), _pair_spec(tm),
                  _const_spec(w_out_a.shape), _const_spec(w_out_b.shape),
                  _const_spec((1, d_model)), _const_spec(w_up.shape), _const_spec(w_dn.shape)],
        out_specs=_tok_spec(tm, d_model),
        out_shape=jax.ShapeDtypeStruct(h.shape, h.dtype),
        compiler_params=_params(2),
        name="outproj_mlp",
    )(h, mix_a, mix_b, w_out_a, w_out_b, gain, w_up, w_dn)


def kernel(x, positions, attn_norm_gain, w_in, q_norm_a, k_norm_a, sinks_a, q_norm_b, k_norm_b,
           w_out, mlp_norm_gain, w_up, w_down):
    batch, seq, d_model = x.shape
    assert seq // DILATIONS[2] == BAND
    assert seq % TM_PROJ == 0 and seq % TM_MLP == 0

    rope_table = _rope_table(positions)

    o_ka, o_va = WQA, WQA + WKA
    o_qb, o_vb = WQA + 2 * WKA, WQA + 2 * WKA + 2 * WB
    qa_cols = np.concatenate([np.arange(h * HEAD_DIM, (h + 1) * HEAD_DIM) for h in HEAD_ORDER_A])
    col_order = np.concatenate([qa_cols, np.arange(o_ka, o_va), np.arange(o_qb, o_vb),
                                np.arange(o_va, o_qb), np.arange(o_vb, o_vb + WB)])
    scale = HEAD_DIM ** -0.5 * LOG2E

    h = x
    for layer in range(w_in.shape[0]):
        qk_gain = jnp.concatenate([
            jnp.tile(q_norm_a[layer] * scale, N_HEADS_A), jnp.tile(k_norm_a[layer], N_KV_A),
            jnp.tile(q_norm_b[layer] * scale, N_HEADS_B), jnp.tile(k_norm_b[layer], N_HEADS_B)])[None]
        qa, ka, va, qb, kb, vb = _project(
            h, rope_table, attn_norm_gain[layer][None], w_in[layer][:, col_order].astype(BF16),
            qk_gain, TM_PROJ)
        mix_a = _mixer_a(sinks_a[layer].astype(F32), qa, ka, va)
        mix_b = _mixer_b(qb, kb, vb)
        h = _out_mlp(h, mix_a, mix_b, w_out[layer][qa_cols].astype(BF16),
                     w_out[layer][WQA:].astype(BF16), mlp_norm_gain[layer][None],
                     w_up[layer].astype(BF16), w_down[layer].astype(BF16), TM_MLP)
    return h
```

```python
import functools
import math

import numpy as np
import jax
import jax.numpy as jnp
from jax import lax
from jax.experimental import pallas as pl
from jax.experimental.pallas import tpu as pltpu

F32 = jnp.float32
BF16 = jnp.bfloat16

LANES = 128
HEAD_DIM = 64
PAIR = 2 * HEAD_DIM
N_HEADS_A = 8
N_KV_A = 2
N_HEADS_B = 8
WINDOW_A = 128
DILATIONS = (1, 4, 16)
BAND = 128
ROPE_DIM = HEAD_DIM // 4
ROPE_THETA = 500000.0
EPS = 1e-6
LOG2E = 1.4426950408889634
TM_PROJ = 1024
TM_MLP = 512
PROJ_CHUNK = 2
PROJ_DEPTH = 2
DEPTH_A = 5
DEPTH_B = 4
VMEM_LIMIT = 56 * 1024 * 1024

WQA = N_HEADS_A * HEAD_DIM
WKA = N_KV_A * HEAD_DIM
WB = N_HEADS_B * HEAD_DIM
QK_WIDTH = WQA + WKA + 2 * WB
N_PAIR_A = WQA // PAIR
N_PAIR_B = WB // PAIR
HEAD_ORDER_A = tuple(h for p in range(N_PAIR_A) for h in (p, p + N_HEADS_A // N_KV_A))


def _lane_lo(shape):
    return lax.broadcasted_iota(jnp.int32, shape, len(shape) - 1) < HEAD_DIM


def _rope_table_kernel(pos_ref, freq_ref, o_ref):
    for i in range(pos_ref.shape[0]):
        ang = freq_ref[...] * pos_ref[i].astype(F32)
        cs = jnp.concatenate([jnp.cos(ang), jnp.sin(ang)], axis=0)
        high = cs.astype(BF16)
        rest = cs - high.astype(F32)
        middle = rest.astype(BF16)
        low = (rest - middle.astype(F32)).astype(BF16)
        row = lax.broadcasted_iota(jnp.int32, cs.shape, 0)
        ones_row = jnp.where(row == 0, 1.0, 0.0).astype(BF16)
        o_ref[i] = jnp.concatenate([high, middle, low, ones_row], axis=0)


def _proj_kernel(x_ref, rope_ref, gain_ref, w_ref, qkgain_ref, ropesel_ref, headsum_ref,
                 qa_ref, ka_ref, va_ref, qb_ref, kb_ref, vb_ref):
    x = x_ref[0]
    ms = jnp.mean(x * x, axis=-1, keepdims=True)
    hn = (x * lax.rsqrt(ms + EPS) * gain_ref[...]).astype(BF16)

    rope = lax.dot_general(rope_ref[0], ropesel_ref[...], (((0,), (0,)), ((), ())),
                           preferred_element_type=F32)
    cos, sin = rope[:, :LANES], rope[:, LANES:]
    tm = x.shape[0]
    lane = lax.broadcasted_iota(jnp.int32, (tm, LANES), 1)
    lo = lane < HEAD_DIM
    first_half = (lane & (HEAD_DIM - 1)) < (ROPE_DIM // 2)

    def qk_norm(p, c0):
        width = p.shape[1]
        mean_sq = jnp.dot((p * p).astype(BF16), headsum_ref[:width, :width],
                          preferred_element_type=F32)
        return p * lax.rsqrt(mean_sq + EPS) * qkgain_ref[:, c0 * LANES:c0 * LANES + width]

    def rope(y):
        partner = jnp.where(first_half,
                            pltpu.roll(y, LANES - ROPE_DIM // 2, 1),
                            pltpu.roll(y, ROPE_DIM // 2, 1))
        return y * cos + partner * sin

    def store_cols(ref, j):
        def store(t):
            ref[0, :, j * LANES:(j + 1) * LANES] = t.astype(ref.dtype)
        return store

    def store_pair(ref, j):
        def store(t):
            ref[0, j] = t
        return store

    def store_split(ref):
        def store(t):
            zero = jnp.zeros_like(t)
            ref[0, :, :LANES] = jnp.where(lo, t, zero).astype(ref.dtype)
            ref[0, :, LANES:] = jnp.where(lo, zero, t).astype(ref.dtype)
        return store

    n_qk = QK_WIDTH // LANES
    stores = ([store_cols(qa_ref, j) for j in range(N_PAIR_A)] + [store_split(ka_ref)]
              + [store_pair(qb_ref, j) for j in range(N_PAIR_B)]
              + [store_pair(kb_ref, j) for j in range(N_PAIR_B)] + [store_split(va_ref)]
              + [store_pair(vb_ref, j) for j in range(N_PAIR_B)])
    chunks = [(c0, min(PROJ_CHUNK, len(stores) - c0)) for c0 in range(0, len(stores), PROJ_CHUNK)]

    def product(chunk):
        c0, n = chunk
        return jnp.dot(hn, w_ref[:, c0 * LANES:(c0 + n) * LANES], preferred_element_type=F32)

    pending = [product(ch) for ch in chunks[:PROJ_DEPTH]]
    for i, (c0, n) in enumerate(chunks):
        p = pending.pop(0)
        nq = max(0, min(n, n_qk - c0))
        if nq:
            y = qk_norm(p[:, :nq * LANES], c0)
        if i + PROJ_DEPTH < len(chunks):
            pending.append(product(chunks[i + PROJ_DEPTH]))
        for t in range(n):
            cols = slice(t * LANES, (t + 1) * LANES)
            stores[c0 + t](rope(y[:, cols]) if t < nq else p[:, cols])


def _split_heads(t):
    lo = _lane_lo(t.shape)
    zero = jnp.zeros_like(t)
    return jnp.where(lo, t, zero).astype(BF16), jnp.where(lo, zero, t).astype(BF16)


def _scores(q, k_lo, k_hi, mask):
    dn = (((1,), (1,)), ((), ()))
    s0 = lax.dot_general(q, k_lo, dn, preferred_element_type=F32)
    s1 = lax.dot_general(q, k_hi, dn, preferred_element_type=F32)
    return jnp.where(mask, s0, -jnp.inf), jnp.where(mask, s1, -jnp.inf)


def _softmax_pair(s0, s1, floor0=None, floor1=None):
    m0 = jnp.max(s0, axis=-1, keepdims=True)
    m1 = jnp.max(s1, axis=-1, keepdims=True)
    if floor0 is not None:
        m0, m1 = jnp.maximum(m0, floor0), jnp.maximum(m1, floor1)
    p0 = jnp.exp2(s0 - m0)
    p1 = jnp.exp2(s1 - m1)
    l0 = jnp.sum(p0, axis=-1, keepdims=True)
    l1 = jnp.sum(p1, axis=-1, keepdims=True)
    return m0, m1, l0, l1, p0.astype(BF16), p1.astype(BF16)


def _weighted_values(p0, p1, v_lo, v_hi):
    return (jnp.dot(p0, v_lo, preferred_element_type=F32)
            + jnp.dot(p1, v_hi, preferred_element_type=F32))


def _band_mask(nq, nk, max_dist):
    qi = lax.broadcasted_iota(jnp.int32, (nq, nk), 0)
    kj = lax.broadcasted_iota(jnp.int32, (nq, nk), 1)
    dist = BAND + qi - kj
    return (dist >= 0) & (dist <= max_dist)


def _causal_mask(n):
    qi = lax.broadcasted_iota(jnp.int32, (n, n), 0)
    kj = lax.broadcasted_iota(jnp.int32, (n, n), 1)
    return kj <= qi


def _run_blocks(blocks, depth):
    def scores(block):
        load_q, load_k, _, mask = block[:4]
        return _scores(load_q().astype(BF16), *load_k(), mask)

    pending = [scores(b) for b in blocks[:depth]]
    for j, (_, _, load_v, _, floors, emit) in enumerate(blocks):
        m0, m1, l0, l1, p0, p1 = _softmax_pair(*pending.pop(0), *floors)
        acc = _weighted_values(p0, p1, *load_v())
        if j + depth < len(blocks):
            pending.append(scores(blocks[j + depth]))
        emit(m0, m1, l0, l1, acc)


def _attn_a_kernel(sink_ref, q_ref, k_ref, v_ref, o_ref):
    seq = q_ref.shape[1]
    lo = _lane_lo((BAND, LANES))
    band = _band_mask(BAND, 2 * BAND, WINDOW_A - 1)
    causal = _causal_mask(BAND)
    sinks = [sink_ref[h] * LOG2E for h in HEAD_ORDER_A]

    blocks = []
    for b in range(seq // BAND):
        q_rows = pl.ds(b * BAND, BAND)
        k_rows = q_rows if b == 0 else pl.ds((b - 1) * BAND, 2 * BAND)
        for p in range(N_PAIR_A):
            cols = slice(p * LANES, (p + 1) * LANES)
            sink0, sink1 = sinks[2 * p], sinks[2 * p + 1]

            def emit(m0, m1, l0, l1, acc, q_rows=q_rows, cols=cols, sink0=sink0, sink1=sink1):
                den = jnp.where(lo, l0, l1) + jnp.exp2(jnp.where(lo, sink0 - m0, sink1 - m1))
                o_ref[0, q_rows, cols] = (acc / den).astype(o_ref.dtype)

            blocks.append((lambda q_rows=q_rows, cols=cols: q_ref[0, q_rows, cols],
                           lambda k_rows=k_rows: (k_ref[0, k_rows, :LANES], k_ref[0, k_rows, LANES:]),
                           lambda k_rows=k_rows: (v_ref[0, k_rows, :LANES], v_ref[0, k_rows, LANES:]),
                           causal if b == 0 else band, (sink0, sink1), emit))
    _run_blocks(blocks, DEPTH_A)


def _attn_b_kernel(q_ref, k_ref, v_ref, o_ref, m2_ref, l2_ref, acc2_ref, m3_ref, l3_ref, acc3_ref):
    seq = q_ref.shape[2]
    lo = _lane_lo((BAND, LANES))
    band = _band_mask(BAND, 2 * BAND, BAND)
    causal = _causal_mask(BAND)

    def block(q_rows, k_rows, mask, emit):
        return (lambda: q_ref[0, 0, q_rows, :], lambda: _split_heads(k_ref[0, 0, k_rows, :]),
                lambda: _split_heads(v_ref[0, 0, k_rows, :]), mask, (), emit)

    def store_to(m_ref, l_ref, acc_ref, rows):
        def emit(m0, m1, l0, l1, acc):
            m_ref[rows, :] = jnp.where(lo, m0, m1)
            l_ref[rows, :] = jnp.where(lo, l0, l1)
            acc_ref[rows, :] = acc
        return emit

    def finish(q_rows):
        def emit(m0, m1, l0, l1, acc1):
            m1t, l1t = jnp.where(lo, m0, m1), jnp.where(lo, l0, l1)
            m2, m3 = m2_ref[q_rows, :], m3_ref[q_rows, :]
            m = jnp.maximum(jnp.maximum(m1t, m2), m3)
            a1, a2, a3 = jnp.exp2(m1t - m), jnp.exp2(m2 - m), jnp.exp2(m3 - m)
            den = l1t * a1 + l2_ref[q_rows, :] * a2 + l3_ref[q_rows, :] * a3
            acc = acc1 * a1 + acc2_ref[q_rows, :] * a2 + acc3_ref[q_rows, :] * a3
            o_ref[0, 0, q_rows, :] = (acc / den).astype(o_ref.dtype)
        return emit

    d2, d3 = DILATIONS[1], DILATIONS[2]

    blocks = []
    for r in range(d3):
        rows = pl.ds(r, seq // d3, stride=d3)
        blocks.append(block(rows, rows, causal, store_to(m3_ref, l3_ref, acc3_ref, rows)))
    for b in range(seq // d2 // BAND):
        for r in range(d2):
            q_rows = pl.ds(b * (BAND * d2) + r, BAND, stride=d2)
            k_rows = q_rows if b == 0 else pl.ds((b - 1) * (BAND * d2) + r, 2 * BAND, stride=d2)
            blocks.append(block(q_rows, k_rows, causal if b == 0 else band,
                                store_to(m2_ref, l2_ref, acc2_ref, q_rows)))
    for b in range(seq // BAND):
        q_rows = pl.ds(b * BAND, BAND)
        k_rows = q_rows if b == 0 else pl.ds((b - 1) * BAND, 2 * BAND)
        blocks.append(block(q_rows, k_rows, causal if b == 0 else band, finish(q_rows)))
    _run_blocks(blocks, DEPTH_B)


def _mlp_kernel(x_ref, mixa_ref, mixb_ref, wo_ref, gain_ref, wup_ref, wdn_ref, o_ref,
                *, ff_chunk):
    mixb = jnp.concatenate([mixb_ref[0, j] for j in range(N_PAIR_B)], axis=-1)
    woa = jnp.concatenate([wo_ref[h * HEAD_DIM:(h + 1) * HEAD_DIM, :] for h in HEAD_ORDER_A], axis=0)
    h = (x_ref[0]
         + jnp.dot(mixa_ref[0], woa, preferred_element_type=F32)
         + jnp.dot(mixb, wo_ref[WQA:, :], preferred_element_type=F32))
    ms = jnp.mean(h * h, axis=-1, keepdims=True)
    hn = (h * lax.rsqrt(ms + EPS) * gain_ref[...]).astype(BF16)
    mlp = None
    d_ff = wup_ref.shape[1]
    for c in range(d_ff // ff_chunk):
        u = jnp.dot(hn, wup_ref[:, c * ff_chunk:(c + 1) * ff_chunk], preferred_element_type=F32)
        a = jnp.square(jnp.maximum(u, 0.0)).astype(BF16)
        d = jnp.dot(a, wdn_ref[c * ff_chunk:(c + 1) * ff_chunk, :], preferred_element_type=F32)
        mlp = d if mlp is None else mlp + d
    o_ref[0] = h + mlp


def _const_spec(shape):
    nd = len(shape)
    return pl.BlockSpec(shape, lambda *_: (0,) * nd, pipeline_mode=pl.Buffered(1))


def _tok_spec(tm, width):
    return pl.BlockSpec((1, tm, width), lambda b, i: (b, i, 0))


def _pair_spec(tm):
    return pl.BlockSpec((1, N_PAIR_B, tm, LANES), lambda b, i: (b, 0, i, 0))


def _params(n_axes):
    return pltpu.CompilerParams(dimension_semantics=("parallel",) * n_axes,
                                vmem_limit_bytes=VMEM_LIMIT)


N_FREQ = ROPE_DIM // 2
ROPE_ROWS = 8 * N_FREQ
ROPE_BATCH = 8


def _rope_table(positions):
    batch, seq = positions.shape
    rows = math.gcd(batch, ROPE_BATCH)
    inv_freq = ROPE_THETA ** (-(np.arange(0, ROPE_DIM, 2, dtype=np.float32) / ROPE_DIM))
    return pl.pallas_call(
        _rope_table_kernel,
        grid=(batch // rows,),
        in_specs=[pl.BlockSpec((rows, 1, seq), lambda b: (b, 0, 0)), _const_spec((N_FREQ, 1))],
        out_specs=pl.BlockSpec((rows, ROPE_ROWS, seq), lambda b: (b, 0, 0)),
        out_shape=jax.ShapeDtypeStruct((batch, ROPE_ROWS, seq), BF16),
        compiler_params=_params(1),
        name="rope_table",
    )(positions.reshape(batch, 1, seq), jnp.asarray(inv_freq[:, None], F32))


def _rope_select():
    sel = np.zeros((ROPE_ROWS, 2 * LANES), np.float32)
    for lane in range(LANES):
        d = lane % HEAD_DIM
        if d < ROPE_DIM:
            for term in range(3):
                sel[term * 2 * N_FREQ + d % N_FREQ, lane] = 1.0
                sel[term * 2 * N_FREQ + N_FREQ + d % N_FREQ, LANES + lane] = -1.0 if d < N_FREQ else 1.0
        else:
            sel[6 * N_FREQ, lane] = 1.0
    return jnp.asarray(sel, BF16)


def _project(h, rope_table, gain, w_in, qk_gain, tm):
    batch, seq, d_model = h.shape
    pair_shape = jax.ShapeDtypeStruct((batch, N_PAIR_B, seq, LANES), F32)
    head = np.arange(2 * LANES) // HEAD_DIM
    head_sum = jnp.asarray((head[:, None] == head[None, :]) / HEAD_DIM, BF16)
    rope_sel = _rope_select()
    return pl.pallas_call(
        _proj_kernel,
        grid=(batch, seq // tm),
        in_specs=[_tok_spec(tm, d_model), pl.BlockSpec((1, ROPE_ROWS, tm), lambda b, i: (b, 0, i)),
                  _const_spec((1, d_model)),
                  _const_spec(w_in.shape), _const_spec((1, QK_WIDTH)),
                  _const_spec(rope_sel.shape), _const_spec(head_sum.shape)],
        out_specs=[_tok_spec(tm, WQA), _tok_spec(tm, 2 * WKA), _tok_spec(tm, 2 * WKA),
                   _pair_spec(tm), _pair_spec(tm), _pair_spec(tm)],
        out_shape=[jax.ShapeDtypeStruct((batch, seq, WQA), BF16),
                   jax.ShapeDtypeStruct((batch, seq, 2 * WKA), BF16),
                   jax.ShapeDtypeStruct((batch, seq, 2 * WKA), BF16),
                   pair_shape, pair_shape, pair_shape],
        compiler_params=_params(2),
        name="proj_qknorm_rope",
    )(h, rope_table, gain, w_in, qk_gain, rope_sel, head_sum)


def _mixer_a(sinks, qa, ka, va):
    batch, seq, _ = qa.shape
    seq_spec = lambda w: pl.BlockSpec((1, seq, w), lambda b: (b, 0, 0))
    return pl.pallas_call(
        _attn_a_kernel,
        grid=(batch,),
        in_specs=[pl.BlockSpec(memory_space=pltpu.SMEM), seq_spec(WQA), seq_spec(2 * WKA),
                  seq_spec(2 * WKA)],
        out_specs=seq_spec(WQA),
        out_shape=jax.ShapeDtypeStruct((batch, seq, WQA), BF16),
        compiler_params=_params(1),
        name="swa_gqa_sinks",
    )(sinks, qa, ka, va)


def _mixer_b(qb, kb, vb):
    batch, _, seq, _ = qb.shape
    head_spec = pl.BlockSpec((1, 1, seq, LANES), lambda b, p: (b, p, 0, 0))
    return pl.pallas_call(
        _attn_b_kernel,
        grid=(batch, N_PAIR_B),
        in_specs=[head_spec, head_spec, head_spec],
        out_specs=head_spec,
        out_shape=jax.ShapeDtypeStruct(qb.shape, BF16),
        scratch_shapes=[pltpu.VMEM((seq, LANES), F32)] * 6,
        compiler_params=_params(2),
        name="dilated_mixture",
    )(qb, kb, vb)


def _out_mlp(h, mix_a, mix_b, w_out, gain, w_up, w_dn, tm):
    batch, seq, d_model = h.shape
    return pl.pallas_call(
        functools.partial(_mlp_kernel, ff_chunk=1024),
        grid=(batch, seq // tm),
        in_specs=[_tok_spec(tm, d_model), _tok_spec(tm, WQA), _pair_spec(tm),
                  _const_spec(w_out.shape),
                  _const_spec((1, d_model)), _const_spec(w_up.shape), _const_spec(w_dn.shape)],
        out_specs=_tok_spec(tm, d_model),
        out_shape=jax.ShapeDtypeStruct(h.shape, h.dtype),
        compiler_params=_params(2),
        name="outproj_mlp",
    )(h, mix_a, mix_b, w_out, gain, w_up, w_dn)


def kernel(x, positions, attn_norm_gain, w_in, q_norm_a, k_norm_a, sinks_a, q_norm_b, k_norm_b,
           w_out, mlp_norm_gain, w_up, w_down):
    batch, seq, d_model = x.shape
    assert seq // DILATIONS[2] == BAND
    assert seq % TM_PROJ == 0 and seq % TM_MLP == 0

    rope_table = _rope_table(positions)

    o_ka, o_va = WQA, WQA + WKA
    o_qb, o_vb = WQA + 2 * WKA, WQA + 2 * WKA + 2 * WB
    qa_cols = np.concatenate([np.arange(h * HEAD_DIM, (h + 1) * HEAD_DIM) for h in HEAD_ORDER_A])
    col_order = np.concatenate([qa_cols, np.arange(o_ka, o_va), np.arange(o_qb, o_vb),
                                np.arange(o_va, o_qb), np.arange(o_vb, o_vb + WB)])
    scale = HEAD_DIM ** -0.5 * LOG2E

    h = x
    for layer in range(w_in.shape[0]):
        qk_gain = jnp.concatenate([
            jnp.tile(q_norm_a[layer] * scale, N_HEADS_A), jnp.tile(k_norm_a[layer], N_KV_A),
            jnp.tile(q_norm_b[layer] * scale, N_HEADS_B), jnp.tile(k_norm_b[layer], N_HEADS_B)])[None]
        qa, ka, va, qb, kb, vb = _project(
            h, rope_table, attn_norm_gain[layer][None], w_in[layer][:, col_order].astype(BF16),
            qk_gain, TM_PROJ)
        mix_a = _mixer_a(sinks_a[layer].astype(F32), qa, ka, va)
        mix_b = _mixer_b(qb, kb, vb)
        h = _out_mlp(h, mix_a, mix_b, w_out[layer].astype(BF16), mlp_norm_gain[layer][None],
                     w_up[layer].astype(BF16), w_down[layer].astype(BF16), TM_MLP)
    return h
```
